```python
import jax, jax.numpy as jnp
from jax import lax
import numpy as np

D_MODEL = 1024
BATCH = 16
SEQ = 4096
DEPTH = 4

HEAD_DIM = 64
GRID_W = 64
D_FOURIER = D_MODEL // 4
D_LRU = 3 * D_MODEL // 8
D_NA = D_MODEL - D_FOURIER - D_LRU
D_MIX = D_FOURIER + D_LRU + D_NA
N_FOURIER_GROUPS = D_FOURIER // HEAD_DIM
N_LRU_HEADS = D_LRU // HEAD_DIM
N_NA_HEADS = D_NA // HEAD_DIM
D_IN = D_FOURIER + 2 * D_LRU + 3 * D_NA
CONV_W = 4
CONV_PAD_LEFT = 2
CONV_PAD_RIGHT = CONV_W - 1 - CONV_PAD_LEFT
LRU_C = 8.0
NA_KH = 8
NA_KW = 16
D_FF = -(-(8 * D_MODEL) // (3 * 256)) * 256
N_MOD = 6
EPS = 1e-6

kernel_name = "hybrid_fnet_rglru_natten_encoder"


def rms_norm(x, g):
    xf = x.astype(jnp.float32)
    y = xf * lax.rsqrt(jnp.mean(xf * xf, axis=-1, keepdims=True) + EPS)
    return (y * g.astype(jnp.float32)).astype(x.dtype)


def modulate(h, shift, scale):
    return h * (1 + scale[:, None, :]) + shift[:, None, :]


def fourier_mix(u, w_map):
    B, S, _ = u.shape
    ug = u.astype(jnp.float32).reshape(B, S, N_FOURIER_GROUPS, HEAD_DIM)
    f = jnp.real(jnp.fft.fft2(ug, axes=(1, 3), norm="ortho")).astype(u.dtype)
    y = jnp.einsum('bsgd,gde->bsge', f, w_map)
    return y.reshape(B, S, D_FOURIER)


def centred_depthwise_conv(u, w, b):
    S = u.shape[1]
    up = jnp.pad(u, ((0, 0), (CONV_PAD_LEFT, CONV_PAD_RIGHT), (0, 0)))
    y = sum(w[k] * up[:, k:k + S, :] for k in range(CONV_W))
    return y + b


def _linear_recurrence_combine(left, right):
    a_l, b_l = left
    a_r, b_r = right
    return a_l * a_r, a_r * b_l + b_r


def rg_lru_direction(u, w_a, b_a, w_x, b_x, lam, reverse):
    B, S, _ = u.shape
    uh = u.reshape(B, S, N_LRU_HEADS, HEAD_DIM)
    ra = jnp.einsum('bshd,hde->bshe', uh, w_a).reshape(B, S, D_LRU) + b_a
    rx = jnp.einsum('bshd,hde->bshe', uh, w_x).reshape(B, S, D_LRU) + b_x
    r = jax.nn.sigmoid(ra.astype(jnp.float32))
    i = jax.nn.sigmoid(rx.astype(jnp.float32))
    log_a = -LRU_C * r * jax.nn.softplus(-lam.astype(jnp.float32))
    a = jnp.exp(log_a)
    bterm = jnp.sqrt(-jnp.expm1(2.0 * log_a)) * (i * u.astype(jnp.float32))
    _, h = lax.associative_scan(_linear_recurrence_combine, (a, bterm), axis=1, reverse=reverse)
    return h


def neighbourhood_attention(q, k, v, rpb):
    B, S, H, dh = q.shape
    rows = S // GRID_W
    kh = min(NA_KH, rows)
    kw = NA_KW
    qg = q.reshape(B, rows, GRID_W, H, dh) * (HEAD_DIM ** -0.5)
    kg = k.reshape(B, rows, GRID_W, H, dh)
    vg = v.reshape(B, rows, GRID_W, H, dh)
    row_start = jnp.clip(jnp.arange(rows) - kh // 2, 0, rows - kh)
    cols = jnp.arange(GRID_W)
    col_start = jnp.clip(cols - kw // 2, 0, GRID_W - kw)
    col_idx = col_start[:, None] + jnp.arange(kw)[None, :]
    dc = col_idx - cols[:, None] + (NA_KW - 1)
    bias_c = rpb[:, :, dc]

    def one_row(r):
        rs = row_start[r]
        k_band = lax.dynamic_slice_in_dim(kg, rs, kh, axis=1)
        v_band = lax.dynamic_slice_in_dim(vg, rs, kh, axis=1)
        k_win = k_band[:, :, col_idx]
        v_win = v_band[:, :, col_idx]
        q_row = lax.dynamic_index_in_dim(qg, r, axis=1, keepdims=False)
        s = jnp.einsum('bqhd,brqkhd->bhqrk', q_row, k_win)
        dr = rs + jnp.arange(kh) - r + (NA_KH - 1)
        bias = jnp.transpose(bias_c[:, dr], (0, 2, 1, 3))
        s = s.astype(jnp.float32) + bias.astype(jnp.float32)
        p = jax.nn.softmax(s.reshape(B, H, GRID_W, kh * kw), axis=-1)
        p = p.reshape(B, H, GRID_W, kh, kw).astype(v.dtype)
        return jnp.einsum('bhqrk,brqkhd->bqhd', p, v_win)

    out = lax.map(one_row, jnp.arange(rows, dtype=jnp.int32))
    return jnp.transpose(out, (1, 0, 2, 3, 4)).reshape(B, S, H * dh)


def setup_inputs(seed: int = 0) -> dict:
    key = jax.random.key(seed)
    ks = jax.random.split(key, 24)
    f32 = jnp.float32

    def nrm(k, shape, scale):
        return jax.random.normal(k, shape, f32) * scale

    u = jax.random.uniform(ks[12], (DEPTH, 2, D_LRU), f32, minval=0.9, maxval=0.999)
    s = u ** (1.0 / LRU_C)
    lam = jnp.log(s) - jnp.log1p(-s)
    return {
        "x": nrm(ks[0], (BATCH, SEQ, D_MODEL), 1.0),
        "c": nrm(ks[1], (BATCH, D_MODEL), 1.0),
        "w_ada": nrm(ks[2], (DEPTH, D_MODEL, N_MOD * D_MODEL), 0.5 * D_MODEL ** -0.5),
        "b_ada": nrm(ks[3], (DEPTH, N_MOD * D_MODEL), 0.02),
        "g_mix": 1.0 + nrm(ks[4], (DEPTH, D_MODEL), 0.02),
        "g_ffn": 1.0 + nrm(ks[5], (DEPTH, D_MODEL), 0.02),
        "w_in": nrm(ks[6], (DEPTH, D_MODEL, D_IN), D_MODEL ** -0.5),
        "w_fourier": nrm(ks[7], (DEPTH, N_FOURIER_GROUPS, HEAD_DIM, HEAD_DIM), HEAD_DIM ** -0.5),
        "conv_w": nrm(ks[8], (DEPTH, CONV_W, D_LRU), 0.5),
        "conv_b": nrm(ks[9], (DEPTH, D_LRU), 0.02),
        "lru_w_a": nrm(ks[10], (DEPTH, 2, N_LRU_HEADS, HEAD_DIM, HEAD_DIM), HEAD_DIM ** -0.5),
        "lru_b_a": nrm(ks[11], (DEPTH, 2, D_LRU), 0.02),
        "lru_w_x": nrm(ks[13], (DEPTH, 2, N_LRU_HEADS, HEAD_DIM, HEAD_DIM), HEAD_DIM ** -0.5),
        "lru_b_x": nrm(ks[14], (DEPTH, 2, D_LRU), 0.02),
        "lru_lambda": lam,
        "na_rpb": nrm(ks[15], (DEPTH, N_NA_HEADS, 2 * NA_KH - 1, 2 * NA_KW - 1), 0.1),
        "w_out": nrm(ks[16], (DEPTH, D_MIX, D_MODEL), D_MIX ** -0.5),
        "w_ffn_gate": nrm(ks[17], (DEPTH, D_MODEL, D_FF), D_MODEL ** -0.5),
        "w_ffn_up": nrm(ks[18], (DEPTH, D_MODEL, D_FF), D_MODEL ** -0.5),
        "w_ffn_down": nrm(ks[19], (DEPTH, D_FF, D_MODEL), D_FF ** -0.5),
        "g_final": 1.0 + nrm(ks[20], (D_MODEL,), 0.02),
    }


def reference(x, c, w_ada, b_ada, g_mix, g_ffn, w_in, w_fourier, conv_w, conv_b,
              lru_w_a, lru_b_a, lru_w_x, lru_b_x, lru_lambda, na_rpb, w_out,
              w_ffn_gate, w_ffn_up, w_ffn_down, g_final):
    B, S, _ = x.shape
    c_act = jax.nn.silu(c)
    o1 = D_FOURIER
    o2 = o1 + D_LRU
    o3 = o2 + D_LRU
    o4 = o3 + D_NA
    o5 = o4 + D_NA
    for l in range(DEPTH):
        mod = c_act @ w_ada[l] + b_ada[l]
        sh1, sc1, gt1, sh2, sc2, gt2 = jnp.split(mod, N_MOD, axis=-1)

        h = modulate(rms_norm(x, g_mix[l]), sh1, sc1)
        p = h @ w_in[l]
        p_f, p_x, p_g = p[..., :o1], p[..., o1:o2], p[..., o2:o3]
        q = p[..., o3:o4].reshape(B, S, N_NA_HEADS, HEAD_DIM)
        k = p[..., o4:o5].reshape(B, S, N_NA_HEADS, HEAD_DIM)
        v = p[..., o5:].reshape(B, S, N_NA_HEADS, HEAD_DIM)

        y_f = fourier_mix(p_f, w_fourier[l])

        u = centred_depthwise_conv(p_x, conv_w[l], conv_b[l])
        h_fwd = rg_lru_direction(u, lru_w_a[l, 0], lru_b_a[l, 0], lru_w_x[l, 0],
                                 lru_b_x[l, 0], lru_lambda[l, 0], reverse=False)
        h_bwd = rg_lru_direction(u, lru_w_a[l, 1], lru_b_a[l, 1], lru_w_x[l, 1],
                                 lru_b_x[l, 1], lru_lambda[l, 1], reverse=True)
        y_l = jax.nn.gelu(p_g) * (h_fwd + h_bwd).astype(p_g.dtype)

        y_n = neighbourhood_attention(q, k, v, na_rpb[l])

        y = jnp.concatenate([y_f, y_l, y_n], axis=-1) @ w_out[l]
        x = x + gt1[:, None, :] * y

        h = modulate(rms_norm(x, g_ffn[l]), sh2, sc2)
        f = (jax.nn.silu(h @ w_ffn_gate[l]) * (h @ w_ffn_up[l])) @ w_ffn_down[l]
        x = x + gt2[:, None, :] * f
    return rms_norm(x, g_final)
```

```python
import functools

import numpy as np
import jax
import jax.numpy as jnp
from jax import lax
from jax.experimental import pallas as pl
from jax.experimental.pallas import tpu as pltpu

F32 = jnp.float32
BF16 = jnp.bfloat16

HEAD_DIM = 64
GRID_W = 64
CONV_W = 4
LRU_C = 8.0
NA_KH = 8
NA_KW = 16
N_MOD = 6
EPS = 1e-6
NEG_BIAS = -1e30

LANES = 128
SUBLANES = 8
VMEM_LIMIT = 56 * 1024 * 1024

TOKEN_TILE = 512
FF_CHUNK = 256
LRU_TILE = 128
LRU_GROUP = 8


def _params(*sem):
    return pltpu.CompilerParams(dimension_semantics=sem, vmem_limit_bytes=VMEM_LIMIT)


def _const_spec(shape):
    nd = len(shape)
    return pl.BlockSpec(shape, lambda *_: (0,) * nd, pipeline_mode=pl.Buffered(1))


def _norm_mod(x, g, shift, scale):
    ms = jnp.mean(x * x, axis=-1, keepdims=True)
    y = x * lax.rsqrt(ms + EPS) * g
    return y * (1.0 + scale) + shift


def _gelu_tanh(x):
    return 0.5 * x * (1.0 + jnp.tanh(0.7978845608028654 * (x + 0.044715 * (x * x * x))))


def _mod_kernel(c_ref, w_ref, b_ref, o_ref):
    c = c_ref[...]
    ca = c * jax.nn.sigmoid(c)
    o_ref[0] = jnp.dot(ca, w_ref[0], precision=lax.Precision.HIGHEST,
                       preferred_element_type=F32) + b_ref[0]


def _modulation(c, w_ada, b_ada):
    depth, d, n = w_ada.shape
    b = c.shape[0]
    tn = 1024
    return pl.pallas_call(
        _mod_kernel,
        grid=(depth, n // tn),
        in_specs=[pl.BlockSpec((b, d), lambda l, j: (0, 0)),
                  pl.BlockSpec((1, d, tn), lambda l, j: (l, 0, j)),
                  pl.BlockSpec((1, 1, tn), lambda l, j: (l, 0, j))],
        out_specs=pl.BlockSpec((1, b, tn), lambda l, j: (l, 0, j)),
        out_shape=jax.ShapeDtypeStruct((depth, b, n), F32),
        compiler_params=_params("arbitrary", "arbitrary"),
        name="adaln_mod",
    )(c, w_ada, b_ada.reshape(depth, 1, n))


def _in_proj_kernel(x_ref, mod_ref, g_ref, w_ref, *out_refs, splits):
    h = _norm_mod(x_ref[0], g_ref[...], mod_ref[0, 0:1, :], mod_ref[0, 1:2, :]).astype(BF16)
    for ref, (lo, hi, scale) in zip(out_refs, splits):
        p = jnp.dot(h, w_ref[:, lo:hi], preferred_element_type=F32)
        if scale != 1.0:
            p = p * scale
        ref[0] = p.astype(ref.dtype)


def _in_proj(x, mod, g, w_in, d_f, d_l, d_n):
    b, s, d = x.shape
    tm = TOKEN_TILE
    o1 = d_f
    o2 = o1 + d_l
    o3 = o2 + d_l
    o4 = o3 + d_n
    o5 = o4 + d_n
    o6 = o5 + d_n
    splits = ((0, o1, 1.0), (o1, o2, 1.0), (o2, o3, 1.0),
              (o3, o4, HEAD_DIM ** -0.5), (o4, o5, 1.0), (o5, o6, 1.0))
    dts = (BF16, F32, F32, BF16, BF16, BF16)
    outs = [jax.ShapeDtypeStruct((b, s, hi - lo), dt) for (lo, hi, _), dt in zip(splits, dts)]
    tok = lambda w: pl.BlockSpec((1, tm, w), lambda i, j: (i, j, 0))
    return pl.pallas_call(
        functools.partial(_in_proj_kernel, splits=splits),
        grid=(b, s // tm),
        in_specs=[tok(d),
                  pl.BlockSpec((1, N_MOD, d), lambda i, j: (i, 0, 0)),
                  _const_spec((1, d)),
                  _const_spec(w_in.shape)],
        out_specs=[tok(hi - lo) for lo, hi, _ in splits],
        out_shape=outs,
        compiler_params=_params("parallel", "arbitrary"),
        name="in_proj",
    )(x, mod, g, w_in)


FOURIER_SLAB = 2 * GRID_W + SUBLANES


def _fourier_tables(n_groups):
    n = GRID_W
    idx = np.arange(n)
    ang1 = 2.0 * np.pi * np.outer(idx, idx) / n
    f1 = np.concatenate([np.cos(ang1), np.sin(ang1)], axis=0)
    c = idx[:, None, None]
    dd = idx[None, :, None]
    bb = idx[None, None, :]
    ang2 = 2.0 * np.pi * bb * (n * dd + c) / (n * n)
    gc, gs = np.cos(ang2), np.sin(ang2)
    g2 = np.concatenate([np.concatenate([gc, -gs], axis=2),
                         np.concatenate([gs, gc], axis=2)], axis=1)
    angd = 2.0 * np.pi * np.outer(np.arange(HEAD_DIM), np.arange(HEAD_DIM)) / HEAD_DIM
    eye = np.eye(n_groups)
    cd = np.kron(eye, np.cos(angd))
    sd = np.kron(eye, np.sin(angd))
    return (jnp.asarray(f1, BF16), jnp.asarray(g2, BF16), jnp.asarray(cd, BF16), jnp.asarray(sd, BF16))


def _fourier_kernel(x_ref, f1_ref, g2_ref, cd_ref, sd_ref, w_ref, o_ref, z_ref, *, width, scale):
    n = GRID_W
    f1 = f1_ref[...]
    for b in range(n):
        slab = jnp.dot(f1, x_ref[0, :, b * width:(b + 1) * width], preferred_element_type=F32)
        for lt in range(width // LANES):
            z_ref[lt, b * FOURIER_SLAB:b * FOURIER_SLAB + 2 * n, :] = slab[:, lt * LANES:(lt + 1) * LANES]
    cd = cd_ref[...]
    sd = sd_ref[...]
    w = w_ref[...]
    for c in range(n):
        z = jnp.concatenate(
            [jnp.concatenate([z_ref[lt, pl.ds(off + c, n, stride=FOURIER_SLAB), :]
                              for lt in range(width // LANES)], axis=1)
             for off in (0, n)], axis=0).astype(BF16)
        y = jnp.dot(g2_ref[c], z, preferred_element_type=F32)
        yc = y[0:n].astype(BF16)
        ys = y[n:2 * n].astype(BF16)
        f = (jnp.dot(yc, cd, preferred_element_type=F32)
             - jnp.dot(ys, sd, preferred_element_type=F32)) * scale
        out = jnp.dot(f.astype(BF16), w, preferred_element_type=F32)
        o_ref[0, :, c * width:(c + 1) * width] = out.astype(o_ref.dtype)


def _fourier_mix(p_f, w_blk, tables):
    b, s, width = p_f.shape
    n = GRID_W
    assert s == n * n, "sequence DFT is factored as 64 x 64"
    f1, g2, cd, sd = tables
    x2 = p_f.reshape(b, n, n * width)
    scale = float(1.0 / np.sqrt(s * HEAD_DIM))
    out = pl.pallas_call(
        functools.partial(_fourier_kernel, width=width, scale=scale),
        grid=(b,),
        in_specs=[pl.BlockSpec((1, n, n * width), lambda i: (i, 0, 0)),
                  _const_spec(f1.shape), _const_spec(g2.shape),
                  _const_spec(cd.shape), _const_spec(sd.shape), _const_spec(w_blk.shape)],
        out_specs=pl.BlockSpec((1, n, n * width), lambda i: (i, 0, 0)),
        out_shape=jax.ShapeDtypeStruct((b, n, n * width), BF16),
        scratch_shapes=[pltpu.VMEM((width // LANES, n * FOURIER_SLAB, LANES), F32)],
        compiler_params=_params("parallel"),
        name="fourier_mix",
    )(x2, f1, g2, cd, sd, w_blk)
    return out.reshape(b, s, width)


LRU_HALO = SUBLANES
LRU_STRIDE = LRU_TILE + SUBLANES


def _neg_expm1(y):
    poly = -y * (1.0 + y * (1.0 / 2 + y * (1.0 / 6 + y * (1.0 / 24 + y * (1.0 / 120 + y * (1.0 / 720))))))
    return jnp.where(y > -0.1, poly, 1.0 - jnp.exp(y))


def _softplus_neg(lam):
    e = jnp.exp(-jnp.abs(lam))
    u = 1.0 + e
    l1p = jnp.where(u == 1.0, e, jnp.log(u) * (e / (u - 1.0)))
    return jnp.maximum(-lam, 0.0) + l1p


def _lru_kernel(xf_ref, xfp_ref, xfn_ref, xb_ref, xbp_ref, xbn_ref,
                cw_ref, cb_ref, wa_ref, ba_ref, wx_ref, bx_ref, lam_ref,
                hf_ref, hb_ref,
                xs_ref, af_ref, bf_ref, ab_ref, bb_ref, sf_ref, sb_ref, cf_ref, cbk_ref):
    ts = LRU_TILE
    n_lt = af_ref.shape[0]
    i = pl.program_id(1)
    nt = pl.num_programs(1)
    ib = nt - 1 - i
    cw = cw_ref[...]
    cb = cb_ref[...]

    def gates(x_ref, xp_ref, xn_ref, first, last, d, a_ref, b_ref):
        sp = _softplus_neg(lam_ref[d:d + 1, :])
        ba = ba_ref[d:d + 1, :]
        bx = bx_ref[d:d + 1, :]

        def per_seq(j, carry):
            xs_ref[0:LRU_HALO, :] = jnp.where(first, 0.0, xp_ref[j])
            xs_ref[LRU_HALO:LRU_HALO + ts, :] = x_ref[j]
            xs_ref[LRU_HALO + ts:2 * LRU_HALO + ts, :] = jnp.where(last, 0.0, xn_ref[j])
            u = cb
            for k in range(CONV_W):
                u = u + cw[k:k + 1, :] * xs_ref[pl.ds(LRU_HALO - 2 + k, ts), :]
            ub = u.astype(BF16)
            ra = jnp.dot(ub, wa_ref[d], preferred_element_type=F32) + ba
            rx = jnp.dot(ub, wx_ref[d], preferred_element_type=F32) + bx
            r = jax.nn.sigmoid(ra)
            ig = jax.nn.sigmoid(rx)
            log_a = (-LRU_C) * r * sp
            a = jnp.exp(log_a)
            bt = jnp.sqrt(_neg_expm1(2.0 * log_a)) * (ig * u)
            off = pl.multiple_of(j * LRU_STRIDE, SUBLANES)
            for lt in range(n_lt):
                a_ref[lt, pl.ds(off, ts), :] = a[:, lt * LANES:(lt + 1) * LANES]
                b_ref[lt, pl.ds(off, ts), :] = bt[:, lt * LANES:(lt + 1) * LANES]
            return carry

        lax.fori_loop(0, LRU_GROUP, per_seq, 0)

    gates(xf_ref, xfp_ref, xfn_ref, i == 0, i == nt - 1, 0, af_ref, bf_ref)
    gates(xb_ref, xbp_ref, xbn_ref, ib == 0, ib == nt - 1, 1, ab_ref, bb_ref)

    @pl.when(i == 0)
    def _():
        cf_ref[...] = jnp.zeros_like(cf_ref)
        cbk_ref[...] = jnp.zeros_like(cbk_ref)

    def step(t, carry):
        hf, hb = carry
        tb = ts - 1 - t
        rf = pl.ds(t, LRU_GROUP, stride=LRU_STRIDE)
        rb = pl.ds(tb, LRU_GROUP, stride=LRU_STRIDE)
        hf = tuple(af_ref[lt, rf, :] * hf[lt] + bf_ref[lt, rf, :] for lt in range(n_lt))
        hb = tuple(ab_ref[lt, rb, :] * hb[lt] + bb_ref[lt, rb, :] for lt in range(n_lt))
        for lt in range(n_lt):
            sf_ref[lt, rf, :] = hf[lt]
            sb_ref[lt, rb, :] = hb[lt]
        return hf, hb

    lanes = lambda lt: slice(lt * LANES, (lt + 1) * LANES)
    init = (tuple(cf_ref[:, lanes(lt)] for lt in range(n_lt)),
            tuple(cbk_ref[:, lanes(lt)] for lt in range(n_lt)))
    hf, hb = lax.fori_loop(0, ts, step, init, unroll=8)
    for lt in range(n_lt):
        cf_ref[:, lanes(lt)] = hf[lt]
        cbk_ref[:, lanes(lt)] = hb[lt]

    for j in range(LRU_GROUP):
        for lt in range(n_lt):
            hf_ref[j, :, lanes(lt)] = sf_ref[lt, j * LRU_STRIDE:j * LRU_STRIDE + ts, :]
            hb_ref[j, :, lanes(lt)] = sb_ref[lt, j * LRU_STRIDE:j * LRU_STRIDE + ts, :]


def _rg_lru(p_x, conv_w, conv_b, w_a, b_a, w_x, b_x, lam):
    b, s, c = p_x.shape
    ts, grp, halo = LRU_TILE, LRU_GROUP, LRU_HALO
    assert b % grp == 0 and s % ts == 0
    nt = s // ts
    hb_per_tile = ts // halo
    n_halo = s // halo
    cur = lambda rev: pl.BlockSpec(
        (grp, ts, c), (lambda g, i: (g, nt - 1 - i, 0)) if rev else (lambda g, i: (g, i, 0)))
    prev = lambda rev: pl.BlockSpec(
        (grp, halo, c),
        lambda g, i: (g, jnp.maximum(((nt - 1 - i) if rev else i) * hb_per_tile - 1, 0), 0))
    nxt = lambda rev: pl.BlockSpec(
        (grp, halo, c),
        lambda g, i: (g, jnp.minimum((((nt - 1 - i) if rev else i) + 1) * hb_per_tile, n_halo - 1), 0))
    rows = grp * LRU_STRIDE
    return pl.pallas_call(
        _lru_kernel,
        grid=(b // grp, nt),
        in_specs=[cur(False), prev(False), nxt(False), cur(True), prev(True), nxt(True),
                  _const_spec(conv_w.shape), _const_spec((1, c)),
                  _const_spec(w_a.shape), _const_spec(b_a.shape),
                  _const_spec(w_x.shape), _const_spec(b_x.shape), _const_spec(lam.shape)],
        out_specs=[cur(False), cur(True)],
        out_shape=[jax.ShapeDtypeStruct((b, s, c), F32)] * 2,
        scratch_shapes=[pltpu.VMEM((ts + 2 * halo, c), F32)]
        + [pltpu.VMEM((c // LANES, rows, LANES), F32)] * 6
        + [pltpu.VMEM((grp, c), F32)] * 2,
        compiler_params=_params("parallel", "arbitrary"),
        name="rg_lru",
    )(p_x, p_x, p_x, p_x, p_x, p_x, conv_w, conv_b.reshape(1, c), w_a, b_a, w_x, b_x, lam)


def _block_diag(w):
    h, dh = w.shape[-3], w.shape[-1]
    eye = jnp.eye(h, dtype=w.dtype)
    full = jnp.einsum('...hde,hg->...hdge', w, eye)
    return full.reshape(w.shape[:-3] + (h * dh, h * dh))


def _na_bias(rpb):
    h = rpb.shape[0]
    w, kh, kw = GRID_W, NA_KH, NA_KW
    qc = np.arange(w)
    cs = np.clip(qc - kw // 2, 0, w - kw)
    kc = np.arange(w)
    inwin = (kc[None, :] >= cs[:, None]) & (kc[None, :] < cs[:, None] + kw)
    dc = np.clip(kc[None, :] - qc[:, None] + (NA_KW - 1), 0, 2 * NA_KW - 2)
    var = np.arange(kh)
    ii = np.arange(kh)
    dr = ii[None, :] - var[:, None] + (NA_KH - 1)
    t = rpb[:, :, dc]
    t = jnp.where(jnp.asarray(inwin)[None, None], t, NEG_BIAS)
    t = t[:, dr]
    t = jnp.transpose(t, (1, 0, 3, 2, 4)).reshape(kh, h // 2, 2 * w, kh * w)
    return t.astype(F32)


def _na_kernel(q_ref, k_ref, v_ref, bias_ref, o_ref, *, rows, n_pairs):
    w = GRID_W
    band = NA_KH * w
    lane = lax.broadcasted_iota(jnp.int32, (w, LANES), 1)
    low = lane < HEAD_DIM

    def body(r, carry):
        rs = jnp.clip(r - NA_KH // 2, 0, rows - NA_KH)
        var = r - rs
        q0 = pl.multiple_of(r * w, w)
        k0 = pl.multiple_of(rs * w, w)
        for pr in range(n_pairs):
            sl = slice(pr * LANES, (pr + 1) * LANES)
            q2 = q_ref[0, pl.ds(q0, w), sl]
            zero = jnp.zeros_like(q2)
            qq = jnp.concatenate([jnp.where(low, q2, zero), jnp.where(low, zero, q2)], axis=0)
            kk = k_ref[0, pl.ds(k0, band), sl]
            vv = v_ref[0, pl.ds(k0, band), sl]
            s = lax.dot_general(qq, kk, (((1,), (1,)), ((), ())), preferred_element_type=F32)
            s = s + bias_ref[var, pr]
            m = jnp.max(s, axis=-1, keepdims=True)
            e = jnp.exp(s - m)
            l = jnp.sum(e, axis=-1, keepdims=True)
            pv = jnp.dot(e.astype(BF16), vv, preferred_element_type=F32) / l
            o = jnp.where(low, pv[0:w], pv[w:2 * w])
            o_ref[0, pl.ds(q0, w), sl] = o.astype(o_ref.dtype)
        return carry

    lax.fori_loop(0, rows, body, 0)


def _neighbourhood_attention(q, k, v, bias):
    b, s, c = q.shape
    rows = s // GRID_W
    assert rows >= NA_KH and c % LANES == 0
    n_pairs = c // LANES
    seq = pl.BlockSpec((1, s, c), lambda i: (i, 0, 0))
    return pl.pallas_call(
        functools.partial(_na_kernel, rows=rows, n_pairs=n_pairs),
        grid=(b,),
        in_specs=[seq, seq, seq, _const_spec(bias.shape)],
        out_specs=seq,
        out_shape=jax.ShapeDtypeStruct((b, s, c), BF16),
        compiler_params=_params("parallel"),
        name="neighbourhood_attn",
    )(q, k, v, bias)


def _out_proj_kernel(yf_ref, hf_ref, hb_ref, pg_ref, yn_ref, x_ref, mod_ref, w_ref, o_ref, *, d_f, d_l):
    yl = (_gelu_tanh(pg_ref[0]) * (hf_ref[0] + hb_ref[0])).astype(BF16)
    y = jnp.dot(yf_ref[0], w_ref[0:d_f, :], preferred_element_type=F32)
    y = y + jnp.dot(yl, w_ref[d_f:d_f + d_l, :], preferred_element_type=F32)
    y = y + jnp.dot(yn_ref[0], w_ref[d_f + d_l:, :], preferred_element_type=F32)
    o_ref[0] = x_ref[0] + mod_ref[0, 2:3, :] * y


def _out_proj(y_f, h_f, h_b, p_g, y_n, x, mod, w_out):
    b, s, d = x.shape
    tm = TOKEN_TILE
    d_f, d_l = y_f.shape[-1], h_f.shape[-1]
    tok = lambda w: pl.BlockSpec((1, tm, w), lambda i, j: (i, j, 0))
    return pl.pallas_call(
        functools.partial(_out_proj_kernel, d_f=d_f, d_l=d_l),
        grid=(b, s // tm),
        in_specs=[tok(d_f), tok(d_l), tok(d_l), tok(d_l), tok(y_n.shape[-1]), tok(d),
                  pl.BlockSpec((1, N_MOD, d), lambda i, j: (i, 0, 0)),
                  _const_spec(w_out.shape)],
        out_specs=tok(d),
        out_shape=jax.ShapeDtypeStruct((b, s, d), F32),
        compiler_params=_params("parallel", "arbitrary"),
        name="out_proj",
    )(y_f, h_f, h_b, p_g, y_n, x, mod, w_out)


def _ffn_kernel(x_ref, mod_ref, g_ref, wg_ref, wu_ref, wd_ref, gf_ref, o_ref, acc_ref, *, d_ff, final):
    x = x_ref[0]
    h = _norm_mod(x, g_ref[...], mod_ref[0, 3:4, :], mod_ref[0, 4:5, :]).astype(BF16)
    for j in range(d_ff // FF_CHUNK):
        cs = slice(j * FF_CHUNK, (j + 1) * FF_CHUNK)
        gt = jnp.dot(h, wg_ref[:, cs], preferred_element_type=F32)
        up = jnp.dot(h, wu_ref[:, cs], preferred_element_type=F32)
        act = (gt * jax.nn.sigmoid(gt) * up).astype(BF16)
        part = jnp.dot(act, wd_ref[cs, :], preferred_element_type=F32)
        if j == 0:
            acc_ref[...] = part
        else:
            acc_ref[...] += part
    y = x + mod_ref[0, 5:6, :] * acc_ref[...]
    if final:
        ms = jnp.mean(y * y, axis=-1, keepdims=True)
        y = y * lax.rsqrt(ms + EPS) * gf_ref[...]
    o_ref[0] = y


def _ffn(x, mod, g, w_gate, w_up, w_down, g_final, final):
    b, s, d = x.shape
    tm = TOKEN_TILE
    d_ff = w_gate.shape[-1]
    assert d_ff % FF_CHUNK == 0
    tok = pl.BlockSpec((1, tm, d), lambda i, j: (i, j, 0))
    return pl.pallas_call(
        functools.partial(_ffn_kernel, d_ff=d_ff, final=final),
        grid=(b, s // tm),
        in_specs=[tok,
                  pl.BlockSpec((1, N_MOD, d), lambda i, j: (i, 0, 0)),
                  _const_spec((1, d)),
                  _const_spec(w_gate.shape), _const_spec(w_up.shape), _const_spec(w_down.shape),
                  _const_spec((1, d))],
        out_specs=tok,
        out_shape=jax.ShapeDtypeStruct((b, s, d), F32),
        scratch_shapes=[pltpu.VMEM((tm, d), F32)],
        compiler_params=_params("parallel", "arbitrary"),
        name="ffn",
    )(x, mod, g, w_gate, w_up, w_down, g_final)


def kernel(x, c, w_ada, b_ada, g_mix, g_ffn, w_in, w_fourier, conv_w, conv_b, lru_w_a, lru_b_a,
           lru_w_x, lru_b_x, lru_lambda, na_rpb, w_out, w_ffn_gate, w_ffn_up, w_ffn_down, g_final):
    b, s, d = x.shape
    depth = w_in.shape[0]
    n_groups = w_fourier.shape[1]
    d_f = n_groups * HEAD_DIM
    d_l = conv_w.shape[-1]
    d_n = na_rpb.shape[1] * HEAD_DIM

    mod = _modulation(c, w_ada, b_ada).reshape(depth, b, N_MOD, d)
    tables = _fourier_tables(n_groups)
    g_final2 = g_final.reshape(1, d)

    for l in range(depth):
        p_f, p_x, p_g, q, k, v = _in_proj(x, mod[l], g_mix[l].reshape(1, d), w_in[l].astype(BF16),
                                          d_f, d_l, d_n)
        y_f = _fourier_mix(p_f, _block_diag(w_fourier[l]).astype(BF16), tables)
        h_f, h_b = _rg_lru(p_x, conv_w[l], conv_b[l],
                           _block_diag(lru_w_a[l]).astype(BF16), lru_b_a[l],
                           _block_diag(lru_w_x[l]).astype(BF16), lru_b_x[l], lru_lambda[l])
        y_n = _neighbourhood_attention(q, k, v, _na_bias(na_rpb[l]))
        x = _out_proj(y_f, h_f, h_b, p_g, y_n, x, mod[l], w_out[l].astype(BF16))
        x = _ffn(x, mod[l], g_ffn[l].reshape(1, d), w_ffn_gate[l].astype(BF16),
                 w_ffn_up[l].astype(BF16), w_ffn_down[l].astype(BF16), g_final2,
                 final=(l == depth - 1))
    return x
```

```python
import functools

import numpy as np
import jax
import jax.numpy as jnp
from jax import lax
from jax.experimental import pallas as pl
from jax.experimental.pallas import tpu as pltpu

F32 = jnp.float32
BF16 = jnp.bfloat16

HEAD_DIM = 64
GRID_W = 64
CONV_W = 4
LRU_C = 8.0
NA_KH = 8
NA_KW = 16
N_MOD = 6
EPS = 1e-6
NEG_BIAS = -1e30
LOG2E = 1.4426950408889634
NA_ROWS_PER_STEP = 2

LANES = 128
SUBLANES = 8
VMEM_LIMIT = 56 * 1024 * 1024

TOKEN_TILE = 512
FF_CHUNK = 256
LRU_TILE = 128
LRU_GROUP = 8


def _params(*sem):
    return pltpu.CompilerParams(dimension_semantics=sem, vmem_limit_bytes=VMEM_LIMIT)


def _const_spec(shape):
    nd = len(shape)
    return pl.BlockSpec(shape, lambda *_: (0,) * nd, pipeline_mode=pl.Buffered(1))


def _norm_mod(x, g, shift, scale):
    ms = jnp.mean(x * x, axis=-1, keepdims=True)
    y = x * lax.rsqrt(ms + EPS) * g
    return y * (1.0 + scale) + shift


def _gelu_tanh(x):
    return 0.5 * x * (1.0 + jnp.tanh(0.7978845608028654 * (x + 0.044715 * (x * x * x))))


def _mod_kernel(c_ref, w_ref, b_ref, o_ref):
    c = c_ref[...]
    ca = c * jax.nn.sigmoid(c)
    o_ref[0] = jnp.dot(ca, w_ref[0], precision=lax.Precision.HIGHEST,
                       preferred_element_type=F32) + b_ref[0]


def _modulation(c, w_ada, b_ada):
    depth, d, n = w_ada.shape
    b = c.shape[0]
    tn = 1024
    return pl.pallas_call(
        _mod_kernel,
        grid=(depth, n // tn),
        in_specs=[pl.BlockSpec((b, d), lambda l, j: (0, 0)),
                  pl.BlockSpec((1, d, tn), lambda l, j: (l, 0, j)),
                  pl.BlockSpec((1, 1, tn), lambda l, j: (l, 0, j))],
        out_specs=pl.BlockSpec((1, b, tn), lambda l, j: (l, 0, j)),
        out_shape=jax.ShapeDtypeStruct((depth, b, n), F32),
        compiler_params=_params("arbitrary", "arbitrary"),
        name="adaln_mod",
    )(c, w_ada, b_ada.reshape(depth, 1, n))


def _in_proj_kernel(x_ref, mod_ref, g_ref, w_ref, *out_refs, splits):
    h = _norm_mod(x_ref[0], g_ref[...], mod_ref[0, 0:1, :], mod_ref[0, 1:2, :]).astype(BF16)
    for ref, (lo, hi, scale) in zip(out_refs, splits):
        p = jnp.dot(h, w_ref[:, lo:hi], preferred_element_type=F32)
        if scale != 1.0:
            p = p * scale
        ref[0] = p.astype(ref.dtype)


def _in_proj(x, mod, g, w_in, d_f, d_l, d_n):
    b, s, d = x.shape
    tm = TOKEN_TILE
    o1 = d_f
    o2 = o1 + d_l
    o3 = o2 + d_l
    o4 = o3 + d_n
    o5 = o4 + d_n
    o6 = o5 + d_n
    splits = ((0, o1, 1.0), (o1, o2, 1.0), (o2, o3, 1.0),
              (o3, o4, HEAD_DIM ** -0.5 * LOG2E), (o4, o5, 1.0), (o5, o6, 1.0))
    dts = (BF16, F32, F32, BF16, BF16, BF16)
    outs = [jax.ShapeDtypeStruct((b, s, hi - lo), dt) for (lo, hi, _), dt in zip(splits, dts)]
    tok = lambda w: pl.BlockSpec((1, tm, w), lambda i, j: (i, j, 0))
    return pl.pallas_call(
        functools.partial(_in_proj_kernel, splits=splits),
        grid=(b, s // tm),
        in_specs=[tok(d),
                  pl.BlockSpec((1, N_MOD, d), lambda i, j: (i, 0, 0)),
                  _const_spec((1, d)),
                  _const_spec(w_in.shape)],
        out_specs=[tok(hi - lo) for lo, hi, _ in splits],
        out_shape=outs,
        compiler_params=_params("parallel", "arbitrary"),
        name="in_proj",
    )(x, mod, g, w_in)


FOURIER_SLAB = 2 * GRID_W + SUBLANES


def _fourier_tables(n_groups):
    n = GRID_W
    idx = np.arange(n)
    ang1 = 2.0 * np.pi * np.outer(idx, idx) / n
    f1 = np.concatenate([np.cos(ang1), np.sin(ang1)], axis=0)
    c = idx[:, None, None]
    dd = idx[None, :, None]
    bb = idx[None, None, :]
    ang2 = 2.0 * np.pi * bb * (n * dd + c) / (n * n)
    gc, gs = np.cos(ang2), np.sin(ang2)
    g2 = np.concatenate([np.concatenate([gc, -gs], axis=2),
                         np.concatenate([gs, gc], axis=2)], axis=1)
    angd = 2.0 * np.pi * np.outer(np.arange(HEAD_DIM), np.arange(HEAD_DIM)) / HEAD_DIM
    eye = np.eye(n_groups)
    cd = np.kron(eye, np.cos(angd))
    sd = np.kron(eye, np.sin(angd))
    return (jnp.asarray(f1, BF16), jnp.asarray(g2, BF16), jnp.asarray(cd, BF16), jnp.asarray(sd, BF16))


def _fourier_kernel(x_ref, f1_ref, g2_ref, cd_ref, sd_ref, w_ref, o_ref, z_ref, *, width, scale):
    n = GRID_W
    f1 = f1_ref[...]
    for b in range(n):
        slab = jnp.dot(f1, x_ref[0, :, b * width:(b + 1) * width], preferred_element_type=F32)
        for lt in range(width // LANES):
            z_ref[lt, b * FOURIER_SLAB:b * FOURIER_SLAB + 2 * n, :] = slab[:, lt * LANES:(lt + 1) * LANES]
    cd = cd_ref[...]
    sd = sd_ref[...]
    w = w_ref[...]
    for c in range(n):
        z = jnp.concatenate(
            [jnp.concatenate([z_ref[lt, pl.ds(off + c, n, stride=FOURIER_SLAB), :]
                              for lt in range(width // LANES)], axis=1)
             for off in (0, n)], axis=0).astype(BF16)
        y = jnp.dot(g2_ref[c], z, preferred_element_type=F32)
        yc = y[0:n].astype(BF16)
        ys = y[n:2 * n].astype(BF16)
        f = (jnp.dot(yc, cd, preferred_element_type=F32)
             - jnp.dot(ys, sd, preferred_element_type=F32)) * scale
        out = jnp.dot(f.astype(BF16), w, preferred_element_type=F32)
        o_ref[0, :, c * width:(c + 1) * width] = out.astype(o_ref.dtype)


def _fourier_mix(p_f, w_blk, tables):
    b, s, width = p_f.shape
    n = GRID_W
    assert s == n * n, "sequence DFT is factored as 64 x 64"
    f1, g2, cd, sd = tables
    x2 = p_f.reshape(b, n, n * width)
    scale = float(1.0 / np.sqrt(s * HEAD_DIM))
    out = pl.pallas_call(
        functools.partial(_fourier_kernel, width=width, scale=scale),
        grid=(b,),
        in_specs=[pl.BlockSpec((1, n, n * width), lambda i: (i, 0, 0)),
                  _const_spec(f1.shape), _const_spec(g2.shape),
                  _const_spec(cd.shape), _const_spec(sd.shape), _const_spec(w_blk.shape)],
        out_specs=pl.BlockSpec((1, n, n * width), lambda i: (i, 0, 0)),
        out_shape=jax.ShapeDtypeStruct((b, n, n * width), BF16),
        scratch_shapes=[pltpu.VMEM((width // LANES, n * FOURIER_SLAB, LANES), F32)],
        compiler_params=_params("parallel"),
        name="fourier_mix",
    )(x2, f1, g2, cd, sd, w_blk)
    return out.reshape(b, s, width)


LRU_HALO = SUBLANES
LRU_XSTRIDE = LRU_TILE + 3 * SUBLANES
LRU_CONV_LEFT = 2
LRU_CHUNK = 32


def _softplus_neg(lam):
    e = jnp.exp(-jnp.abs(lam))
    u = 1.0 + e
    l1p = jnp.where(u == 1.0, e, jnp.log(u) * (e / (u - 1.0)))
    return jnp.maximum(-lam, 0.0) + l1p


def _lru_kernel(xf_ref, xfp_ref, xfn_ref, xb_ref, xbp_ref, xbn_ref,
                cw_ref, cb_ref, wa_ref, ba_ref, wx_ref, bx_ref, lam_ref,
                hf_ref, hb_ref,
                xs_ref, xt_ref, a_ref, b_ref, h_ref, cf_ref, cbk_ref):
    ts, grp = LRU_TILE, LRU_GROUP
    n_lt = xs_ref.shape[0]
    lanes = lambda lt: slice(lt * LANES, (lt + 1) * LANES)
    i = pl.program_id(1)
    nt = pl.num_programs(1)
    cw = cw_ref[...]
    cb = cb_ref[...]

    @pl.when(i == 0)
    def _():
        cf_ref[...] = jnp.zeros_like(cf_ref)
        cbk_ref[...] = jnp.zeros_like(cbk_ref)

    def direction(x_ref, xp_ref, xn_ref, o_ref, c_ref, ti, d):
        first = ti == 0
        last = ti == nt - 1
        for j in range(grp):
            base = j * LRU_XSTRIDE
            for lt in range(n_lt):
                xs_ref[lt, base:base + LRU_HALO, :] = jnp.where(first, 0.0, xp_ref[j, :, lanes(lt)])
                xs_ref[lt, base + LRU_HALO:base + LRU_HALO + ts, :] = x_ref[j, :, lanes(lt)]
                xs_ref[lt, base + LRU_HALO + ts:base + 2 * LRU_HALO + ts, :] = \
                    jnp.where(last, 0.0, xn_ref[j, :, lanes(lt)])

        def to_time_major(tt, carry):
            r0 = pl.multiple_of(tt * grp, grp)
            src = pl.ds(LRU_HALO - LRU_CONV_LEFT + tt, grp, stride=LRU_XSTRIDE)
            for lt in range(n_lt):
                xt_ref[pl.ds(r0, grp), lanes(lt)] = xs_ref[lt, src, :]
            return carry

        lax.fori_loop(0, ts + CONV_W - 1, to_time_major, 0, unroll=8)

        sp4 = (-0.5 * LRU_C) * _softplus_neg(lam_ref[d:d + 1, :])
        ba = ba_ref[d:d + 1, :]
        bx = bx_ref[d:d + 1, :]
        rows_c = LRU_CHUNK * grp

        def gate_chunk(ch, carry):
            r0 = pl.multiple_of(ch * rows_c, rows_c)
            u = cb
            for k in range(CONV_W):
                u = u + cw[k:k + 1, :] * xt_ref[pl.ds(r0 + k * grp, rows_c), :]
            ub = u.astype(BF16)
            tha = jnp.tanh(0.5 * (jnp.dot(ub, wa_ref[d], preferred_element_type=F32) + ba))
            thx = jnp.tanh(0.5 * (jnp.dot(ub, wx_ref[d], preferred_element_type=F32) + bx))
            log_a = sp4 * (tha + 1.0)
            a = jnp.exp(log_a)
            m = -jnp.tanh(log_a) * (1.0 + a * a)
            sq = jnp.where(m > 0.0, m * lax.rsqrt(m), 0.0)
            a_ref[pl.ds(r0, rows_c), :] = a
            b_ref[pl.ds(r0, rows_c), :] = sq * ((0.5 * thx + 0.5) * u)
            return carry

        lax.fori_loop(0, ts // LRU_CHUNK, gate_chunk, 0)

        def step(s, h):
            t = (ts - 1 - s) if d == 1 else s
            rows = pl.ds(pl.multiple_of(t * grp, grp), grp)
            h = tuple(a_ref[rows, lanes(lt)] * h[lt] + b_ref[rows, lanes(lt)] for lt in range(n_lt))
            for lt in range(n_lt):
                h_ref[lt, rows, :] = h[lt]
            return h

        h = lax.fori_loop(0, ts, step, tuple(c_ref[:, lanes(lt)] for lt in range(n_lt)), unroll=8)
        for lt in range(n_lt):
            c_ref[:, lanes(lt)] = h[lt]

        for j in range(grp):
            for tb in range(ts // SUBLANES):
                src = pl.ds(tb * SUBLANES * grp + j, SUBLANES, stride=grp)
                for lt in range(n_lt):
                    o_ref[j, tb * SUBLANES:(tb + 1) * SUBLANES, lanes(lt)] = h_ref[lt, src, :]

    direction(xf_ref, xfp_ref, xfn_ref, hf_ref, cf_ref, i, 0)
    direction(xb_ref, xbp_ref, xbn_ref, hb_ref, cbk_ref, nt - 1 - i, 1)


def _rg_lru(p_x, conv_w, conv_b, w_a, b_a, w_x, b_x, lam):
    b, s, c = p_x.shape
    ts, grp, halo = LRU_TILE, LRU_GROUP, LRU_HALO
    assert b % grp == 0 and s % ts == 0
    nt = s // ts
    hb_per_tile = ts // halo
    n_halo = s // halo
    cur = lambda rev: pl.BlockSpec(
        (grp, ts, c), (lambda g, i: (g, nt - 1 - i, 0)) if rev else (lambda g, i: (g, i, 0)))
    prev = lambda rev: pl.BlockSpec(
        (grp, halo, c),
        lambda g, i: (g, jnp.maximum(((nt - 1 - i) if rev else i) * hb_per_tile - 1, 0), 0))
    nxt = lambda rev: pl.BlockSpec(
        (grp, halo, c),
        lambda g, i: (g, jnp.minimum((((nt - 1 - i) if rev else i) + 1) * hb_per_tile, n_halo - 1), 0))
    assert grp == SUBLANES and ts % LRU_CHUNK == 0 and c % LANES == 0
    n_lt = c // LANES
    return pl.pallas_call(
        _lru_kernel,
        grid=(b // grp, nt),
        in_specs=[cur(False), prev(False), nxt(False), cur(True), prev(True), nxt(True),
                  _const_spec(conv_w.shape), _const_spec((1, c)),
                  _const_spec(w_a.shape), _const_spec(b_a.shape),
                  _const_spec(w_x.shape), _const_spec(b_x.shape), _const_spec(lam.shape)],
        out_specs=[cur(False), cur(True)],
        out_shape=[jax.ShapeDtypeStruct((b, s, c), F32)] * 2,
        scratch_shapes=[pltpu.VMEM((n_lt, grp * LRU_XSTRIDE, LANES), F32),
                        pltpu.VMEM(((ts + CONV_W - 1) * grp, c), F32),
                        pltpu.VMEM((ts * grp, c), F32),
                        pltpu.VMEM((ts * grp, c), F32),
                        pltpu.VMEM((n_lt, ts * grp, LANES), F32),
                        pltpu.VMEM((grp, c), F32), pltpu.VMEM((grp, c), F32)],
        compiler_params=_params("parallel", "arbitrary"),
        name="rg_lru",
    )(p_x, p_x, p_x, p_x, p_x, p_x, conv_w, conv_b.reshape(1, c), w_a, b_a, w_x, b_x, lam)


def _block_diag(w):
    h, dh = w.shape[-3], w.shape[-1]
    eye = jnp.eye(h, dtype=w.dtype)
    full = jnp.einsum('...hde,hg->...hdge', w, eye)
    return full.reshape(w.shape[:-3] + (h * dh, h * dh))


def _na_bias(rpb):
    h = rpb.shape[0]
    w, kh, kw = GRID_W, NA_KH, NA_KW
    qc = np.arange(w)
    cs = np.clip(qc - kw // 2, 0, w - kw)
    kc = np.arange(w)
    inwin = (kc[None, :] >= cs[:, None]) & (kc[None, :] < cs[:, None] + kw)
    dc = np.clip(kc[None, :] - qc[:, None] + (NA_KW - 1), 0, 2 * NA_KW - 2)
    var = np.arange(kh)
    ii = np.arange(kh)
    dr = ii[None, :] - var[:, None] + (NA_KH - 1)
    t = rpb[:, :, dc]
    t = jnp.where(jnp.asarray(inwin)[None, None], t * LOG2E, NEG_BIAS)
    t = t[:, dr]
    t = jnp.transpose(t, (1, 0, 3, 2, 4)).reshape(kh, h // 2, 2 * w, kh * w)
    return t.astype(F32)


def _na_kernel(q_ref, k_ref, v_ref, bias_ref, o_ref, *, rows, n_pairs):
    w = GRID_W
    band = NA_KH * w
    lane = lax.broadcasted_iota(jnp.int32, (w, LANES), 1)
    low = lane < HEAD_DIM

    def body(it, carry):
        chains = []
        for rr in range(NA_ROWS_PER_STEP):
            r = it * NA_ROWS_PER_STEP + rr
            rs = jnp.clip(r - NA_KH // 2, 0, rows - NA_KH)
            q0 = pl.multiple_of(r * w, w)
            k0 = pl.multiple_of(rs * w, w)
            for pr in range(n_pairs):
                chains.append((r - rs, q0, k0, pr, slice(pr * LANES, (pr + 1) * LANES)))
        scores = []
        for var, q0, k0, pr, sl in chains:
            q2 = q_ref[0, pl.ds(q0, w), sl]
            zero = jnp.zeros_like(q2)
            qq = jnp.concatenate([jnp.where(low, q2, zero), jnp.where(low, zero, q2)], axis=0)
            kk = k_ref[0, pl.ds(k0, band), sl]
            s = lax.dot_general(qq, kk, (((1,), (1,)), ((), ())), preferred_element_type=F32)
            scores.append(s + bias_ref[var, pr])
        probs = []
        for s in scores:
            e = jnp.exp2(s - jnp.max(s, axis=-1, keepdims=True))
            probs.append((e.astype(BF16), jnp.sum(e, axis=-1, keepdims=True)))
        for (var, q0, k0, pr, sl), (e, l) in zip(chains, probs):
            vv = v_ref[0, pl.ds(k0, band), sl]
            pv = jnp.dot(e, vv, preferred_element_type=F32) / l
            o = jnp.where(low, pv[0:w], pv[w:2 * w])
            o_ref[0, pl.ds(q0, w), sl] = o.astype(o_ref.dtype)
        return carry

    lax.fori_loop(0, rows // NA_ROWS_PER_STEP, body, 0)


def _neighbourhood_attention(q, k, v, bias):
    b, s, c = q.shape
    rows = s // GRID_W
    assert rows >= NA_KH and rows % NA_ROWS_PER_STEP == 0 and c % LANES == 0
    n_pairs = c // LANES
    seq = pl.BlockSpec((1, s, c), lambda i: (i, 0, 0))
    return pl.pallas_call(
        functools.partial(_na_kernel, rows=rows, n_pairs=n_pairs),
        grid=(b,),
        in_specs=[seq, seq, seq, _const_spec(bias.shape)],
        out_specs=seq,
        out_shape=jax.ShapeDtypeStruct((b, s, c), BF16),
        compiler_params=_params("parallel"),
        name="neighbourhood_attn",
    )(q, k, v, bias)


def _out_proj_kernel(yf_ref, hf_ref, hb_ref, pg_ref, yn_ref, x_ref, mod_ref, w_ref, o_ref, *, d_f, d_l):
    yl = (_gelu_tanh(pg_ref[0]) * (hf_ref[0] + hb_ref[0])).astype(BF16)
    y = jnp.dot(yf_ref[0], w_ref[0:d_f, :], preferred_element_type=F32)
    y = y + jnp.dot(yl, w_ref[d_f:d_f + d_l, :], preferred_element_type=F32)
    y = y + jnp.dot(yn_ref[0], w_ref[d_f + d_l:, :], preferred_element_type=F32)
    o_ref[0] = x_ref[0] + mod_ref[0, 2:3, :] * y


def _out_proj(y_f, h_f, h_b, p_g, y_n, x, mod, w_out):
    b, s, d = x.shape
    tm = TOKEN_TILE
    d_f, d_l = y_f.shape[-1], h_f.shape[-1]
    tok = lambda w: pl.BlockSpec((1, tm, w), lambda i, j: (i, j, 0))
    return pl.pallas_call(
        functools.partial(_out_proj_kernel, d_f=d_f, d_l=d_l),
        grid=(b, s // tm),
        in_specs=[tok(d_f), tok(d_l), tok(d_l), tok(d_l), tok(y_n.shape[-1]), tok(d),
                  pl.BlockSpec((1, N_MOD, d), lambda i, j: (i, 0, 0)),
                  _const_spec(w_out.shape)],
        out_specs=tok(d),
        out_shape=jax.ShapeDtypeStruct((b, s, d), F32),
        compiler_params=_params("parallel", "arbitrary"),
        name="out_proj",
    )(y_f, h_f, h_b, p_g, y_n, x, mod, w_out)


def _ffn_kernel(x_ref, mod_ref, g_ref, wg_ref, wu_ref, wd_ref, gf_ref, o_ref, acc_ref, *, d_ff, final):
    x = x_ref[0]
    h = _norm_mod(x, g_ref[...], mod_ref[0, 3:4, :], mod_ref[0, 4:5, :]).astype(BF16)
    for j in range(d_ff // FF_CHUNK):
        cs = slice(j * FF_CHUNK, (j + 1) * FF_CHUNK)
        gt = jnp.dot(h, wg_ref[:, cs], preferred_element_type=F32)
        up = jnp.dot(h, wu_ref[:, cs], preferred_element_type=F32)
        act = (gt * jax.nn.sigmoid(gt) * up).astype(BF16)
        part = jnp.dot(act, wd_ref[cs, :], preferred_element_type=F32)
        if j == 0:
            acc_ref[...] = part
        else:
            acc_ref[...] += part
    y = x + mod_ref[0, 5:6, :] * acc_ref[...]
    if final:
        ms = jnp.mean(y * y, axis=-1, keepdims=True)
        y = y * lax.rsqrt(ms + EPS) * gf_ref[...]
    o_ref[0] = y


def _ffn(x, mod, g, w_gate, w_up, w_down, g_final, final):
    b, s, d = x.shape
    tm = TOKEN_TILE
    d_ff = w_gate.shape[-1]
    assert d_ff % FF_CHUNK == 0
    tok = pl.BlockSpec((1, tm, d), lambda i, j: (i, j, 0))
    return pl.pallas_call(
        functools.partial(_ffn_kernel, d_ff=d_ff, final=final),
        grid=(b, s // tm),
        in_specs=[tok,
                  pl.BlockSpec((1, N_MOD, d), lambda i, j: (i, 0, 0)),
                  _const_spec((1, d)),
                  _const_spec(w_gate.shape), _const_spec(w_up.shape), _const_spec(w_down.shape),
                  _const_spec((1, d))],
        out_specs=tok,
        out_shape=jax.ShapeDtypeStruct((b, s, d), F32),
        scratch_shapes=[pltpu.VMEM((tm, d), F32)],
        compiler_params=_params("parallel", "arbitrary"),
        name="ffn",
    )(x, mod, g, w_gate, w_up, w_down, g_final)


def kernel(x, c, w_ada, b_ada, g_mix, g_ffn, w_in, w_fourier, conv_w, conv_b, lru_w_a, lru_b_a,
           lru_w_x, lru_b_x, lru_lambda, na_rpb, w_out, w_ffn_gate, w_ffn_up, w_ffn_down, g_final):
    b, s, d = x.shape
    depth = w_in.shape[0]
    n_groups = w_fourier.shape[1]
    d_f = n_groups * HEAD_DIM
    d_l = conv_w.shape[-1]
    d_n = na_rpb.shape[1] * HEAD_DIM

    mod = _modulation(c, w_ada, b_ada).reshape(depth, b, N_MOD, d)
    tables = _fourier_tables(n_groups)
    g_final2 = g_final.reshape(1, d)

    for l in range(depth):
        p_f, p_x, p_g, q, k, v = _in_proj(x, mod[l], g_mix[l].reshape(1, d), w_in[l].astype(BF16),
                                          d_f, d_l, d_n)
        y_f = _fourier_mix(p_f, _block_diag(w_fourier[l]).astype(BF16), tables)
        h_f, h_b = _rg_lru(p_x, conv_w[l], conv_b[l],
                           _block_diag(lru_w_a[l]).astype(BF16), lru_b_a[l],
                           _block_diag(lru_w_x[l]).astype(BF16), lru_b_x[l], lru_lambda[l])
        y_n = _neighbourhood_attention(q, k, v, _na_bias(na_rpb[l]))
        x = _out_proj(y_f, h_f, h_b, p_g, y_n, x, mod[l], w_out[l].astype(BF16))
        x = _ffn(x, mod[l], g_ffn[l].reshape(1, d), w_ffn_gate[l].astype(BF16),
                 w_ffn_up[l].astype(BF16), w_ffn_down[l].astype(BF16), g_final2,
                 final=(l == depth - 1))
    return x
```

```python
import functools

import numpy as np
import jax
import jax.numpy as jnp
from jax import lax
from jax.experimental import pallas as pl
from jax.experimental.pallas import tpu as pltpu

F32 = jnp.float32
BF16 = jnp.bfloat16

HEAD_DIM = 64
GRID_W = 64
CONV_W = 4
LRU_C = 8.0
NA_KH = 8
NA_KW = 16
N_MOD = 6
EPS = 1e-6
NEG_BIAS = -1e30
LOG2E = 1.4426950408889634
NA_ROWS_PER_STEP = 2

LANES = 128
SUBLANES = 8
VMEM_LIMIT = 56 * 1024 * 1024
TAIL_VMEM_LIMIT = 60 * 1024 * 1024

TOKEN_TILE = 512
FF_CHUNK = 256
LRU_TILE = 128
LRU_GROUP = 8


def _params(*sem):
    return pltpu.CompilerParams(dimension_semantics=sem, vmem_limit_bytes=VMEM_LIMIT)


def _const_spec(shape):
    nd = len(shape)
    return pl.BlockSpec(shape, lambda *_: (0,) * nd, pipeline_mode=pl.Buffered(1))


def _norm_mod(x, g, shift, scale):
    ms = jnp.mean(x * x, axis=-1, keepdims=True)
    y = x * lax.rsqrt(ms + EPS) * g
    return y * (1.0 + scale) + shift


def _gelu_tanh(x):
    return 0.5 * x * (1.0 + jnp.tanh(0.7978845608028654 * (x + 0.044715 * (x * x * x))))


def _mod_kernel(c_ref, w_ref, b_ref, o_ref):
    c = c_ref[...]
    ca = c * jax.nn.sigmoid(c)
    o_ref[0] = jnp.dot(ca, w_ref[0], precision=lax.Precision.HIGHEST,
                       preferred_element_type=F32) + b_ref[0]


def _modulation(c, w_ada, b_ada):
    depth, d, n = w_ada.shape
    b = c.shape[0]
    tn = 1024
    return pl.pallas_call(
        _mod_kernel,
        grid=(depth, n // tn),
        in_specs=[pl.BlockSpec((b, d), lambda l, j: (0, 0)),
                  pl.BlockSpec((1, d, tn), lambda l, j: (l, 0, j)),
                  pl.BlockSpec((1, 1, tn), lambda l, j: (l, 0, j))],
        out_specs=pl.BlockSpec((1, b, tn), lambda l, j: (l, 0, j)),
        out_shape=jax.ShapeDtypeStruct((depth, b, n), F32),
        compiler_params=_params("arbitrary", "arbitrary"),
        name="adaln_mod",
    )(c, w_ada, b_ada.reshape(depth, 1, n))


ROW_PITCH = GRID_W + SUBLANES


def _in_proj_splits(d_f, d_l, d_n):
    o1 = d_f
    o2 = o1 + d_l
    o3 = o2 + d_l
    o4 = o3 + d_n
    o5 = o4 + d_n
    o6 = o5 + d_n
    return ((0, o1, 1.0), (o1, o2, 1.0), (o2, o3, 1.0),
            (o3, o4, HEAD_DIM ** -0.5 * LOG2E), (o4, o5, 1.0), (o5, o6, 1.0))


def _in_proj_out(b, s, tm, splits):
    (lo, hi, _), rest = splits[0], splits[1:]
    rows = s // GRID_W
    shapes = [jax.ShapeDtypeStruct((b, (hi - lo) // LANES, rows * ROW_PITCH, LANES), F32)]
    specs = [pl.BlockSpec((1, (hi - lo) // LANES, tm // GRID_W * ROW_PITCH, LANES),
                          lambda i, j: (i, 0, j, 0))]
    for lo, hi, _ in rest:
        shapes.append(jax.ShapeDtypeStruct((b, s, hi - lo), BF16))
        specs.append(pl.BlockSpec((1, tm, hi - lo), lambda i, j: (i, j, 0)))
    return shapes, specs


def _emit_in_proj(h, w_ref, out_refs, splits):
    pf_ref, rest = out_refs[0], out_refs[1:]
    lo, hi, _ = splits[0]
    p = jnp.dot(h, w_ref[:, lo:hi], preferred_element_type=F32)
    pad = jnp.zeros((ROW_PITCH - GRID_W, LANES), F32)
    for lt in range((hi - lo) // LANES):
        for a in range(h.shape[0] // GRID_W):
            pf_ref[0, lt, a * ROW_PITCH:a * ROW_PITCH + GRID_W, :] = \
                p[a * GRID_W:(a + 1) * GRID_W, lt * LANES:(lt + 1) * LANES]
            pf_ref[0, lt, a * ROW_PITCH + GRID_W:(a + 1) * ROW_PITCH, :] = pad
    for ref, (lo, hi, scale) in zip(rest, splits[1:]):
        p = jnp.dot(h, w_ref[:, lo:hi], preferred_element_type=F32)
        if scale != 1.0:
            p = p * scale
        ref[0] = p.astype(ref.dtype)


def _in_proj_kernel(x_ref, mod_ref, g_ref, w_ref, *out_refs, splits):
    h = _norm_mod(x_ref[0], g_ref[...], mod_ref[0, 0:1, :], mod_ref[0, 1:2, :]).astype(BF16)
    _emit_in_proj(h, w_ref, out_refs, splits)


def _in_proj(x, mod, g, w_in, splits):
    b, s, d = x.shape
    tm = TOKEN_TILE
    shapes, specs = _in_proj_out(b, s, tm, splits)
    return pl.pallas_call(
        functools.partial(_in_proj_kernel, splits=splits),
        grid=(b, s // tm),
        in_specs=[pl.BlockSpec((1, tm, d), lambda i, j: (i, j, 0)),
                  pl.BlockSpec((1, N_MOD, d), lambda i, j: (i, 0, 0)),
                  _const_spec((1, d)),
                  _const_spec(w_in.shape)],
        out_specs=specs,
        out_shape=shapes,
        compiler_params=_params("parallel", "arbitrary"),
        name="in_proj",
    )(x, mod, g, w_in)


FOURIER_SLAB = 2 * GRID_W + SUBLANES


def _fourier_tables(n_groups):
    n = GRID_W
    idx = np.arange(n)
    ang1 = 2.0 * np.pi * np.outer(idx, idx) / n
    f1 = np.concatenate([np.cos(ang1), np.sin(ang1)], axis=0)
    c = idx[:, None, None]
    dd = idx[None, :, None]
    bb = idx[None, None, :]
    ang2 = 2.0 * np.pi * bb * (n * dd + c) / (n * n)
    gc, gs = np.cos(ang2), np.sin(ang2)
    g2 = np.concatenate([np.concatenate([gc, -gs], axis=2),
                         np.concatenate([gs, gc], axis=2)], axis=1)
    angd = 2.0 * np.pi * np.outer(np.arange(HEAD_DIM), np.arange(HEAD_DIM)) / HEAD_DIM
    eye = np.eye(n_groups)
    cd = np.kron(eye, np.cos(angd))
    sd = np.kron(eye, np.sin(angd))
    return (jnp.asarray(f1, BF16), jnp.asarray(g2, BF16), jnp.asarray(cd, BF16), jnp.asarray(sd, BF16))


def _fourier_kernel(x_ref, f1_ref, g2_ref, cd_ref, sd_ref, w_ref, o_ref, z_ref, *, width, scale):
    n = GRID_W
    n_lt = width // LANES
    f1 = f1_ref[...]
    for b in range(n):
        xb = jnp.concatenate([x_ref[0, lt, pl.ds(b, n, stride=ROW_PITCH), :] for lt in range(n_lt)],
                             axis=1).astype(BF16)
        slab = jnp.dot(f1, xb, preferred_element_type=F32)
        for lt in range(n_lt):
            z_ref[lt, b * FOURIER_SLAB:b * FOURIER_SLAB + 2 * n, :] = slab[:, lt * LANES:(lt + 1) * LANES]
    cd = cd_ref[...]
    sd = sd_ref[...]
    w = w_ref[...]
    for c in range(n):
        z = jnp.concatenate(
            [jnp.concatenate([z_ref[lt, pl.ds(off + c, n, stride=FOURIER_SLAB), :]
                              for lt in range(width // LANES)], axis=1)
             for off in (0, n)], axis=0).astype(BF16)
        y = jnp.dot(g2_ref[c], z, preferred_element_type=F32)
        yc = y[0:n].astype(BF16)
        ys = y[n:2 * n].astype(BF16)
        f = (jnp.dot(yc, cd, preferred_element_type=F32)
             - jnp.dot(ys, sd, preferred_element_type=F32)) * scale
        out = jnp.dot(f.astype(BF16), w, preferred_element_type=F32)
        for lt in range(n_lt):
            o_ref[0, lt, pl.ds(c, n, stride=ROW_PITCH), :] = out[:, lt * LANES:(lt + 1) * LANES]
    pad = jnp.zeros((ROW_PITCH - n, LANES), F32)
    for lt in range(n_lt):
        for dd in range(n):
            o_ref[0, lt, dd * ROW_PITCH + n:(dd + 1) * ROW_PITCH, :] = pad


def _fourier_mix(p_f, w_blk, tables):
    b, n_lt, prow, _ = p_f.shape
    n = GRID_W
    assert prow == n * ROW_PITCH, "sequence DFT is factored as 64 x 64"
    width = n_lt * LANES
    f1, g2, cd, sd = tables
    scale = float(1.0 / np.sqrt(n * n * HEAD_DIM))
    blk = pl.BlockSpec((1, n_lt, prow, LANES), lambda i: (i, 0, 0, 0))
    return pl.pallas_call(
        functools.partial(_fourier_kernel, width=width, scale=scale),
        grid=(b,),
        in_specs=[blk,
                  _const_spec(f1.shape), _const_spec(g2.shape),
                  _const_spec(cd.shape), _const_spec(sd.shape), _const_spec(w_blk.shape)],
        out_specs=blk,
        out_shape=jax.ShapeDtypeStruct(p_f.shape, F32),
        scratch_shapes=[pltpu.VMEM((n_lt, n * FOURIER_SLAB, LANES), F32)],
        compiler_params=_params("parallel"),
        name="fourier_mix",
    )(p_f, f1, g2, cd, sd, w_blk)


LRU_HALO = 2 * SUBLANES
LRU_XSTRIDE = LRU_TILE + 2 * LRU_HALO + SUBLANES
LRU_CONV_LEFT = 2
LRU_CHUNK = 32


def _softplus_neg(lam):
    e = jnp.exp(-jnp.abs(lam))
    u = 1.0 + e
    l1p = jnp.where(u == 1.0, e, jnp.log(u) * (e / (u - 1.0)))
    return jnp.maximum(-lam, 0.0) + l1p


def _lru_kernel(xf_ref, xfp_ref, xfn_ref, xb_ref, xbp_ref, xbn_ref,
                cw_ref, cb_ref, wa_ref, ba_ref, wx_ref, bx_ref, lam_ref,
                hf_ref, hb_ref,
                xs_ref, xt_ref, a_ref, b_ref, h_ref, cf_ref, cbk_ref):
    ts, grp = LRU_TILE, LRU_GROUP
    n_lt = xs_ref.shape[0]
    lanes = lambda lt: slice(lt * LANES, (lt + 1) * LANES)
    i = pl.program_id(1)
    nt = pl.num_programs(1)
    cw = cw_ref[...]
    cb = cb_ref[...]

    @pl.when(i == 0)
    def _():
        cf_ref[...] = jnp.zeros_like(cf_ref)
        cbk_ref[...] = jnp.zeros_like(cbk_ref)

    def direction(x_ref, xp_ref, xn_ref, o_ref, c_ref, ti, d):
        first = ti == 0
        last = ti == nt - 1
        for j in range(grp):
            base = j * LRU_XSTRIDE
            for lt in range(n_lt):
                xs_ref[lt, base:base + LRU_HALO, :] = \
                    jnp.where(first, 0.0, xp_ref[j, :, lanes(lt)].astype(F32))
                xs_ref[lt, base + LRU_HALO:base + LRU_HALO + ts, :] = x_ref[j, :, lanes(lt)].astype(F32)
                xs_ref[lt, base + LRU_HALO + ts:base + 2 * LRU_HALO + ts, :] = \
                    jnp.where(last, 0.0, xn_ref[j, :, lanes(lt)].astype(F32))

        def to_time_major(tt, carry):
            r0 = pl.multiple_of(tt * grp, grp)
            src = pl.ds(LRU_HALO - LRU_CONV_LEFT + tt, grp, stride=LRU_XSTRIDE)
            for lt in range(n_lt):
                xt_ref[pl.ds(r0, grp), lanes(lt)] = xs_ref[lt, src, :]
            return carry

        lax.fori_loop(0, ts + CONV_W - 1, to_time_major, 0, unroll=8)

        sp4 = (0.5 * LRU_C) * _softplus_neg(lam_ref[d:d + 1, :])
        ba = 0.5 * ba_ref[d:d + 1, :]
        bx = 0.5 * bx_ref[d:d + 1, :]
        rows_c = LRU_CHUNK * grp

        def gate_chunk(ch, carry):
            r0 = pl.multiple_of(ch * rows_c, rows_c)
            u = cb
            for k in range(CONV_W):
                u = u + cw[k:k + 1, :] * xt_ref[pl.ds(r0 + k * grp, rows_c), :]
            ub = u.astype(BF16)
            tha = jnp.tanh(jnp.dot(ub, wa_ref[d], preferred_element_type=F32) + ba)
            thx = jnp.tanh(jnp.dot(ub, wx_ref[d], preferred_element_type=F32) + bx)
            nla = sp4 * (tha + 1.0)
            a = jnp.exp2(nla * (-LOG2E))
            m = jnp.tanh(nla) * (1.0 + a * a)
            sq = jnp.where(m > 0.0, m * lax.rsqrt(m), 0.0)
            a_ref[pl.ds(r0, rows_c), :] = a
            b_ref[pl.ds(r0, rows_c), :] = sq * ((0.5 * thx + 0.5) * u)
            return carry

        lax.fori_loop(0, ts // LRU_CHUNK, gate_chunk, 0)

        def step(s, h):
            t = (ts - 1 - s) if d == 1 else s
            rows = pl.ds(pl.multiple_of(t * grp, grp), grp)
            h = tuple(a_ref[rows, lanes(lt)] * h[lt] + b_ref[rows, lanes(lt)] for lt in range(n_lt))
            for lt in range(n_lt):
                h_ref[lt, rows, :] = h[lt]
            return h

        h = lax.fori_loop(0, ts, step, tuple(c_ref[:, lanes(lt)] for lt in range(n_lt)), unroll=8)
        for lt in range(n_lt):
            c_ref[:, lanes(lt)] = h[lt]

        pack = 2 * SUBLANES
        for j in range(grp):
            for tb in range(ts // pack):
                for lt in range(n_lt):
                    o_ref[j, tb * pack:(tb + 1) * pack, lanes(lt)] = jnp.concatenate(
                        [h_ref[lt, pl.ds((tb * pack + k * SUBLANES) * grp + j, SUBLANES, stride=grp), :]
                         for k in range(2)], axis=0).astype(o_ref.dtype)

    direction(xf_ref, xfp_ref, xfn_ref, hf_ref, cf_ref, i, 0)
    direction(xb_ref, xbp_ref, xbn_ref, hb_ref, cbk_ref, nt - 1 - i, 1)


def _rg_lru(p_x, conv_w, conv_b, w_a, b_a, w_x, b_x, lam):
    b, s, c = p_x.shape
    ts, grp, halo = LRU_TILE, LRU_GROUP, LRU_HALO
    assert b % grp == 0 and s % ts == 0
    nt = s // ts
    hb_per_tile = ts // halo
    n_halo = s // halo
    cur = lambda rev: pl.BlockSpec(
        (grp, ts, c), (lambda g, i: (g, nt - 1 - i, 0)) if rev else (lambda g, i: (g, i, 0)))
    prev = lambda rev: pl.BlockSpec(
        (grp, halo, c),
        lambda g, i: (g, jnp.maximum(((nt - 1 - i) if rev else i) * hb_per_tile - 1, 0), 0))
    nxt = lambda rev: pl.BlockSpec(
        (grp, halo, c),
        lambda g, i: (g, jnp.minimum((((nt - 1 - i) if rev else i) + 1) * hb_per_tile, n_halo - 1), 0))
    assert grp == SUBLANES and ts % LRU_CHUNK == 0 and c % LANES == 0
    n_lt = c // LANES
    return pl.pallas_call(
        _lru_kernel,
        grid=(b // grp, nt),
        in_specs=[cur(False), prev(False), nxt(False), cur(True), prev(True), nxt(True),
                  _const_spec(conv_w.shape), _const_spec((1, c)),
                  _const_spec(w_a.shape), _const_spec(b_a.shape),
                  _const_spec(w_x.shape), _const_spec(b_x.shape), _const_spec(lam.shape)],
        out_specs=[cur(False), cur(True)],
        out_shape=[jax.ShapeDtypeStruct((b, s, c), BF16)] * 2,
        scratch_shapes=[pltpu.VMEM((n_lt, grp * LRU_XSTRIDE, LANES), F32),
                        pltpu.VMEM(((ts + CONV_W - 1) * grp, c), F32),
                        pltpu.VMEM((ts * grp, c), F32),
                        pltpu.VMEM((ts * grp, c), F32),
                        pltpu.VMEM((n_lt, ts * grp, LANES), F32),
                        pltpu.VMEM((grp, c), F32), pltpu.VMEM((grp, c), F32)],
        compiler_params=_params("parallel", "arbitrary"),
        name="rg_lru",
    )(p_x, p_x, p_x, p_x, p_x, p_x, conv_w, conv_b.reshape(1, c), w_a, b_a, w_x, b_x, lam)


def _block_diag(w):
    h, dh = w.shape[-3], w.shape[-1]
    eye = jnp.eye(h, dtype=w.dtype)
    full = jnp.einsum('...hde,hg->...hdge', w, eye)
    return full.reshape(w.shape[:-3] + (h * dh, h * dh))


def _na_bias(rpb):
    h = rpb.shape[0]
    w, kh, kw = GRID_W, NA_KH, NA_KW
    qc = np.arange(w)
    cs = np.clip(qc - kw // 2, 0, w - kw)
    kc = np.arange(w)
    inwin = (kc[None, :] >= cs[:, None]) & (kc[None, :] < cs[:, None] + kw)
    dc = np.clip(kc[None, :] - qc[:, None] + (NA_KW - 1), 0, 2 * NA_KW - 2)
    var = np.arange(kh)
    ii = np.arange(kh)
    dr = ii[None, :] - var[:, None] + (NA_KH - 1)
    t = rpb[:, :, dc]
    t = jnp.where(jnp.asarray(inwin)[None, None], t * LOG2E, NEG_BIAS)
    t = t[:, dr]
    t = jnp.transpose(t, (1, 0, 3, 2, 4)).reshape(kh, h // 2, 2 * w, kh * w)
    return t.astype(F32)


def _na_kernel(q_ref, k_ref, v_ref, bias_ref, o_ref, *, rows, n_pairs):
    w = GRID_W
    band = NA_KH * w
    lane = lax.broadcasted_iota(jnp.int32, (w, LANES), 1)
    low = lane < HEAD_DIM

    def body(it, carry):
        chains = []
        for rr in range(NA_ROWS_PER_STEP):
            r = it * NA_ROWS_PER_STEP + rr
            rs = jnp.clip(r - NA_KH // 2, 0, rows - NA_KH)
            q0 = pl.multiple_of(r * w, w)
            k0 = pl.multiple_of(rs * w, w)
            for pr in range(n_pairs):
                chains.append((r - rs, q0, k0, pr, slice(pr * LANES, (pr + 1) * LANES)))
        scores = []
        for var, q0, k0, pr, sl in chains:
            q2 = q_ref[0, pl.ds(q0, w), sl]
            zero = jnp.zeros_like(q2)
            qq = jnp.concatenate([jnp.where(low, q2, zero), jnp.where(low, zero, q2)], axis=0)
            kk = k_ref[0, pl.ds(k0, band), sl]
            s = lax.dot_general(qq, kk, (((1,), (1,)), ((), ())), preferred_element_type=F32)
            scores.append(s + bias_ref[var, pr])
        probs = []
        for s in scores:
            e = jnp.exp2(s - jnp.max(s, axis=-1, keepdims=True))
            probs.append((e.astype(BF16), jnp.sum(e, axis=-1, keepdims=True)))
        for (var, q0, k0, pr, sl), (e, l) in zip(chains, probs):
            vv = v_ref[0, pl.ds(k0, band), sl]
            pv = jnp.dot(e, vv, preferred_element_type=F32) / l
            o = jnp.where(low, pv[0:w], pv[w:2 * w])
            o_ref[0, pl.ds(q0, w), sl] = o.astype(o_ref.dtype)
        return carry

    lax.fori_loop(0, rows // NA_ROWS_PER_STEP, body, 0)


def _neighbourhood_attention(q, k, v, bias):
    b, s, c = q.shape
    rows = s // GRID_W
    assert rows >= NA_KH and rows % NA_ROWS_PER_STEP == 0 and c % LANES == 0
    n_pairs = c // LANES
    seq = pl.BlockSpec((1, s, c), lambda i: (i, 0, 0))
    return pl.pallas_call(
        functools.partial(_na_kernel, rows=rows, n_pairs=n_pairs),
        grid=(b,),
        in_specs=[seq, seq, seq, _const_spec(bias.shape)],
        out_specs=seq,
        out_shape=jax.ShapeDtypeStruct((b, s, c), BF16),
        compiler_params=_params("parallel"),
        name="neighbourhood_attn",
    )(q, k, v, bias)


def _tail_kernel(yf_ref, hf_ref, hb_ref, pg_ref, yn_ref, x_ref, mod_ref, wo_ref,
                 g_ref, wg_ref, wu_ref, wd_ref, *rest, d_ff, splits):
    if splits is None:
        gf_ref, o_ref, acc_ref = rest
    else:
        modn_ref, gn_ref, win_ref, o_ref = rest[:4]
        p_refs, acc_ref = rest[4:-1], rest[-1]
    tm = x_ref.shape[1]
    n_lt, d_l = yf_ref.shape[1], hf_ref.shape[2]
    d_f = n_lt * LANES

    yl = (_gelu_tanh(pg_ref[0].astype(F32))
          * (hf_ref[0].astype(F32) + hb_ref[0].astype(F32))).astype(BF16)
    y = jnp.dot(yl, wo_ref[d_f:d_f + d_l, :], preferred_element_type=F32)
    y = y + jnp.dot(yn_ref[0], wo_ref[d_f + d_l:, :], preferred_element_type=F32)
    for lt in range(n_lt):
        yf = jnp.concatenate([yf_ref[0, lt, a * ROW_PITCH:a * ROW_PITCH + GRID_W, :]
                              for a in range(tm // GRID_W)], axis=0).astype(BF16)
        y = y + jnp.dot(yf, wo_ref[lt * LANES:(lt + 1) * LANES, :], preferred_element_type=F32)
    o_ref[0] = x_ref[0] + mod_ref[0, 2:3, :] * y

    h = _norm_mod(o_ref[0], g_ref[...], mod_ref[0, 3:4, :], mod_ref[0, 4:5, :]).astype(BF16)
    for j in range(d_ff // FF_CHUNK):
        cs = slice(j * FF_CHUNK, (j + 1) * FF_CHUNK)
        gt = jnp.dot(h, wg_ref[:, cs], preferred_element_type=F32)
        up = jnp.dot(h, wu_ref[:, cs], preferred_element_type=F32)
        act = (gt * jax.nn.sigmoid(gt) * up).astype(BF16)
        part = jnp.dot(act, wd_ref[cs, :], preferred_element_type=F32)
        if j == 0:
            acc_ref[...] = part
        else:
            acc_ref[...] += part
    x = o_ref[0] + mod_ref[0, 5:6, :] * acc_ref[...]
    if splits is None:
        ms = jnp.mean(x * x, axis=-1, keepdims=True)
        o_ref[0] = x * lax.rsqrt(ms + EPS) * gf_ref[...]
    else:
        o_ref[0] = x
        hn = _norm_mod(x, gn_ref[...], modn_ref[0, 0:1, :], modn_ref[0, 1:2, :]).astype(BF16)
        _emit_in_proj(hn, win_ref, p_refs, splits)


def _layer_tail(y_f, h_f, h_b, p_g, y_n, x, mod, w_out, g_ffn, w_gate, w_up, w_down,
                nxt=None, g_final=None, splits=None):
    b, s, d = x.shape
    tm = TOKEN_TILE
    d_ff = w_gate.shape[-1]
    assert d_ff % FF_CHUNK == 0 and tm % GRID_W == 0
    tok = lambda w: pl.BlockSpec((1, tm, w), lambda i, j: (i, j, 0))
    mod_spec = pl.BlockSpec((1, N_MOD, d), lambda i, j: (i, 0, 0))
    in_specs = [pl.BlockSpec((1, y_f.shape[1], tm // GRID_W * ROW_PITCH, LANES), lambda i, j: (i, 0, j, 0)),
                tok(h_f.shape[-1]), tok(h_b.shape[-1]), tok(p_g.shape[-1]), tok(y_n.shape[-1]), tok(d),
                mod_spec, _const_spec(w_out.shape), _const_spec((1, d)),
                _const_spec(w_gate.shape), _const_spec(w_up.shape), _const_spec(w_down.shape)]
    args = [y_f, h_f, h_b, p_g, y_n, x, mod, w_out, g_ffn, w_gate, w_up, w_down]
    out_shape = [jax.ShapeDtypeStruct((b, s, d), F32)]
    out_specs = [tok(d)]
    if nxt is None:
        in_specs += [_const_spec((1, d))]
        args += [g_final]
        splits = None
    else:
        mod_n, g_n, w_in_n = nxt
        in_specs += [mod_spec, _const_spec((1, d)), _const_spec(w_in_n.shape)]
        args += [mod_n, g_n, w_in_n]
        shapes, specs = _in_proj_out(b, s, tm, splits)
        out_shape += shapes
        out_specs += specs
    return pl.pallas_call(
        functools.partial(_tail_kernel, d_ff=d_ff, splits=splits),
        grid=(b, s // tm),
        in_specs=in_specs,
        out_specs=out_specs,
        out_shape=out_shape,
        scratch_shapes=[pltpu.VMEM((tm, d), F32)],
        compiler_params=pltpu.CompilerParams(dimension_semantics=("parallel", "arbitrary"),
                                             vmem_limit_bytes=TAIL_VMEM_LIMIT),
        name="layer_tail",
    )(*args)


def kernel(x, c, w_ada, b_ada, g_mix, g_ffn, w_in, w_fourier, conv_w, conv_b, lru_w_a, lru_b_a,
           lru_w_x, lru_b_x, lru_lambda, na_rpb, w_out, w_ffn_gate, w_ffn_up, w_ffn_down, g_final):
    b, s, d = x.shape
    depth = w_in.shape[0]
    n_groups = w_fourier.shape[1]
    d_f = n_groups * HEAD_DIM
    d_l = conv_w.shape[-1]
    d_n = na_rpb.shape[1] * HEAD_DIM

    mod = _modulation(c, w_ada, b_ada).reshape(depth, b, N_MOD, d)
    tables = _fourier_tables(n_groups)
    splits = _in_proj_splits(d_f, d_l, d_n)
    w_in_b = w_in.astype(BF16)

    p_f, p_x, p_g, q, k, v = _in_proj(x, mod[0], g_mix[0].reshape(1, d), w_in_b[0], splits)
    for l in range(depth):
        y_f = _fourier_mix(p_f, _block_diag(w_fourier[l]).astype(BF16), tables)
        h_f, h_b = _rg_lru(p_x, conv_w[l], conv_b[l],
                           _block_diag(0.5 * lru_w_a[l]).astype(BF16), lru_b_a[l],
                           _block_diag(0.5 * lru_w_x[l]).astype(BF16), lru_b_x[l], lru_lambda[l])
        y_n = _neighbourhood_attention(q, k, v, _na_bias(na_rpb[l]))
        common = (y_f, h_f, h_b, p_g, y_n, x, mod[l], w_out[l].astype(BF16), g_ffn[l].reshape(1, d),
                  w_ffn_gate[l].astype(BF16), w_ffn_up[l].astype(BF16), w_ffn_down[l].astype(BF16))
        if l + 1 < depth:
            x, p_f, p_x, p_g, q, k, v = _layer_tail(
                *common, nxt=(mod[l + 1], g_mix[l + 1].reshape(1, d), w_in_b[l + 1]), splits=splits)
        else:
            (x,) = _layer_tail(*common, g_final=g_final.reshape(1, d))
    return x
```

```python
import functools

import numpy as np
import jax
import jax.numpy as jnp
from jax import lax
from jax.experimental import pallas as pl
from jax.experimental.pallas import tpu as pltpu

F32 = jnp.float32
BF16 = jnp.bfloat16

HEAD_DIM = 64
GRID_W = 64
CONV_W = 4
LRU_C = 8.0
NA_KH = 8
NA_KW = 16
N_MOD = 6
EPS = 1e-6
NEG_BIAS = -1e30
LOG2E = 1.4426950408889634
NA_ROWS_PER_STEP = 2

LANES = 128
SUBLANES = 8
VMEM_LIMIT = 56 * 1024 * 1024
TAIL_VMEM_LIMIT = 60 * 1024 * 1024

TOKEN_TILE = 512
FF_CHUNK = 256
LRU_TILE = 128
LRU_GROUP = 8


def _params(*sem):
    return pltpu.CompilerParams(dimension_semantics=sem, vmem_limit_bytes=VMEM_LIMIT)


def _const_spec(shape):
    nd = len(shape)
    return pl.BlockSpec(shape, lambda *_: (0,) * nd, pipeline_mode=pl.Buffered(1))


def _norm_mod(x, g, shift, scale):
    ms = jnp.mean(x * x, axis=-1, keepdims=True)
    y = x * lax.rsqrt(ms + EPS) * g
    return y * (1.0 + scale) + shift


def _gelu_tanh(x):
    return 0.5 * x * (1.0 + jnp.tanh(0.7978845608028654 * (x + 0.044715 * (x * x * x))))


def _mod_kernel(c_ref, w_ref, b_ref, o_ref):
    c = c_ref[...]
    ca = c * jax.nn.sigmoid(c)
    o_ref[0] = jnp.dot(ca, w_ref[0], precision=lax.Precision.HIGHEST,
                       preferred_element_type=F32) + b_ref[0]


def _modulation(c, w_ada, b_ada):
    depth, d, n = w_ada.shape
    b = c.shape[0]
    tn = 1024
    return pl.pallas_call(
        _mod_kernel,
        grid=(depth, n // tn),
        in_specs=[pl.BlockSpec((b, d), lambda l, j: (0, 0)),
                  pl.BlockSpec((1, d, tn), lambda l, j: (l, 0, j)),
                  pl.BlockSpec((1, 1, tn), lambda l, j: (l, 0, j))],
        out_specs=pl.BlockSpec((1, b, tn), lambda l, j: (l, 0, j)),
        out_shape=jax.ShapeDtypeStruct((depth, b, n), F32),
        compiler_params=_params("arbitrary", "arbitrary"),
        name="adaln_mod",
    )(c, w_ada, b_ada.reshape(depth, 1, n))


ROW_PITCH = GRID_W + SUBLANES


def _in_proj_splits(d_f, d_l, d_n):
    o1 = d_f
    o2 = o1 + d_l
    o3 = o2 + d_l
    o4 = o3 + d_n
    o5 = o4 + d_n
    o6 = o5 + d_n
    return ((0, o1, 1.0), (o1, o2, 1.0), (o2, o3, 1.0),
            (o3, o4, HEAD_DIM ** -0.5 * LOG2E), (o4, o5, 1.0), (o5, o6, 1.0))


def _in_proj_out(b, s, tm, splits):
    (lo, hi, _), rest = splits[0], splits[1:]
    rows = s // GRID_W
    shapes = [jax.ShapeDtypeStruct((b, (hi - lo) // LANES, rows * ROW_PITCH, LANES), F32)]
    specs = [pl.BlockSpec((1, (hi - lo) // LANES, tm // GRID_W * ROW_PITCH, LANES),
                          lambda i, j: (i, 0, j, 0))]
    for lo, hi, _ in rest:
        shapes.append(jax.ShapeDtypeStruct((b, s, hi - lo), BF16))
        specs.append(pl.BlockSpec((1, tm, hi - lo), lambda i, j: (i, j, 0)))
    return shapes, specs


IN_PROJ_GROUPS = ((1, 2), (3, 4), (5,))


def _emit_in_proj(h, w_ref, out_refs, splits):
    pf_ref = out_refs[0]
    lo, hi, _ = splits[0]
    p = jnp.dot(h, w_ref[:, lo:hi], preferred_element_type=F32)
    pad = jnp.zeros((ROW_PITCH - GRID_W, LANES), F32)
    for lt in range((hi - lo) // LANES):
        for a in range(h.shape[0] // GRID_W):
            pf_ref[0, lt, a * ROW_PITCH:a * ROW_PITCH + GRID_W, :] = \
                p[a * GRID_W:(a + 1) * GRID_W, lt * LANES:(lt + 1) * LANES]
            pf_ref[0, lt, a * ROW_PITCH + GRID_W:(a + 1) * ROW_PITCH, :] = pad
    for group in IN_PROJ_GROUPS:
        g_lo, g_hi = splits[group[0]][0], splits[group[-1]][1]
        p = jnp.dot(h, w_ref[:, g_lo:g_hi], preferred_element_type=F32)
        for idx in group:
            lo, hi, scale = splits[idx]
            part = p[:, lo - g_lo:hi - g_lo]
            if scale != 1.0:
                part = part * scale
            out_refs[idx][0] = part.astype(out_refs[idx].dtype)


def _in_proj_kernel(x_ref, mod_ref, g_ref, w_ref, *out_refs, splits):
    h = _norm_mod(x_ref[0], g_ref[...], mod_ref[0, 0:1, :], mod_ref[0, 1:2, :]).astype(BF16)
    _emit_in_proj(h, w_ref, out_refs, splits)


def _in_proj(x, mod, g, w_in, splits):
    b, s, d = x.shape
    tm = TOKEN_TILE
    shapes, specs = _in_proj_out(b, s, tm, splits)
    return pl.pallas_call(
        functools.partial(_in_proj_kernel, splits=splits),
        grid=(b, s // tm),
        in_specs=[pl.BlockSpec((1, tm, d), lambda i, j: (i, j, 0)),
                  pl.BlockSpec((1, N_MOD, d), lambda i, j: (i, 0, 0)),
                  _const_spec((1, d)),
                  _const_spec(w_in.shape)],
        out_specs=specs,
        out_shape=shapes,
        compiler_params=_params("parallel", "arbitrary"),
        name="in_proj",
    )(x, mod, g, w_in)


FOURIER_SLAB = 2 * GRID_W + SUBLANES
FOURIER_COLS = 8


def _fourier_tables(n_groups):
    n = GRID_W
    idx = np.arange(n)
    ang1 = 2.0 * np.pi * np.outer(idx, idx) / n
    f1 = np.concatenate([np.cos(ang1), np.sin(ang1)], axis=0)
    c = idx[:, None, None]
    dd = idx[None, :, None]
    bb = idx[None, None, :]
    ang2 = 2.0 * np.pi * bb * (n * dd + c) / (n * n)
    gc, gs = np.cos(ang2), np.sin(ang2)
    g2 = np.concatenate([np.concatenate([gc, -gs], axis=2),
                         np.concatenate([gs, gc], axis=2)], axis=1)
    angd = 2.0 * np.pi * np.outer(np.arange(HEAD_DIM), np.arange(HEAD_DIM)) / HEAD_DIM
    eye = np.eye(n_groups)
    cs = np.concatenate([np.kron(eye, np.cos(angd)), -np.kron(eye, np.sin(angd))], axis=0)
    return jnp.asarray(f1, BF16), jnp.asarray(g2, BF16), jnp.asarray(cs, BF16)


def _fourier_kernel(x_ref, f1_ref, g2_ref, cs_ref, w_ref, o_ref, z_ref, y_ref, *, width, scale):
    n = GRID_W
    n_lt = width // LANES
    f1 = f1_ref[...]
    for b in range(n):
        xb = jnp.concatenate([x_ref[0, lt, pl.ds(b, n, stride=ROW_PITCH), :] for lt in range(n_lt)],
                             axis=1).astype(BF16)
        slab = jnp.dot(f1, xb, preferred_element_type=F32)
        for lt in range(n_lt):
            z_ref[lt, b * FOURIER_SLAB:b * FOURIER_SLAB + 2 * n, :] = slab[:, lt * LANES:(lt + 1) * LANES]
    for c in range(n):
        z = jnp.concatenate(
            [jnp.concatenate([z_ref[lt, pl.ds(off + c, n, stride=FOURIER_SLAB), :]
                              for lt in range(n_lt)], axis=1)
             for off in (0, n)], axis=0).astype(BF16)
        y = jnp.dot(g2_ref[c], z, preferred_element_type=F32)
        y_ref[c * n:(c + 1) * n, 0:width] = y[0:n].astype(BF16)
        y_ref[c * n:(c + 1) * n, width:2 * width] = y[n:2 * n].astype(BF16)
    cs = cs_ref[...]
    w = w_ref[...]
    for blk in range(n // FOURIER_COLS):
        rows = slice(blk * FOURIER_COLS * n, (blk + 1) * FOURIER_COLS * n)
        f = jnp.dot(y_ref[rows, :], cs, preferred_element_type=F32) * scale
        out = jnp.dot(f.astype(BF16), w, preferred_element_type=F32)
        for cc in range(FOURIER_COLS):
            c = blk * FOURIER_COLS + cc
            for lt in range(n_lt):
                o_ref[0, lt, pl.ds(c, n, stride=ROW_PITCH), :] = \
                    out[cc * n:(cc + 1) * n, lt * LANES:(lt + 1) * LANES]
    pad = jnp.zeros((ROW_PITCH - n, LANES), F32)
    for lt in range(n_lt):
        for dd in range(n):
            o_ref[0, lt, dd * ROW_PITCH + n:(dd + 1) * ROW_PITCH, :] = pad


def _fourier_mix(p_f, w_blk, tables):
    b, n_lt, prow, _ = p_f.shape
    n = GRID_W
    assert prow == n * ROW_PITCH, "sequence DFT is factored as 64 x 64"
    width = n_lt * LANES
    f1, g2, cs = tables
    scale = float(1.0 / np.sqrt(n * n * HEAD_DIM))
    blk = pl.BlockSpec((1, n_lt, prow, LANES), lambda i: (i, 0, 0, 0))
    return pl.pallas_call(
        functools.partial(_fourier_kernel, width=width, scale=scale),
        grid=(b,),
        in_specs=[blk,
                  _const_spec(f1.shape), _const_spec(g2.shape),
                  _const_spec(cs.shape), _const_spec(w_blk.shape)],
        out_specs=blk,
        out_shape=jax.ShapeDtypeStruct(p_f.shape, F32),
        scratch_shapes=[pltpu.VMEM((n_lt, n * FOURIER_SLAB, LANES), F32),
                        pltpu.VMEM((n * n, 2 * width), BF16)],
        compiler_params=_params("parallel"),
        name="fourier_mix",
    )(p_f, f1, g2, cs, w_blk)


LRU_HALO = 2 * SUBLANES
LRU_XSTRIDE = LRU_TILE + 2 * LRU_HALO + SUBLANES
LRU_CONV_LEFT = 2
LRU_CHUNK = 32


def _softplus_neg(lam):
    e = jnp.exp(-jnp.abs(lam))
    u = 1.0 + e
    l1p = jnp.where(u == 1.0, e, jnp.log(u) * (e / (u - 1.0)))
    return jnp.maximum(-lam, 0.0) + l1p


def _lru_kernel(xf_ref, xfp_ref, xfn_ref, xb_ref, xbp_ref, xbn_ref,
                cw_ref, cb_ref, wa_ref, ba_ref, wx_ref, bx_ref, lam_ref,
                hf_ref, hb_ref,
                xs_ref, xt_ref, a_ref, b_ref, h_ref, cf_ref, cbk_ref):
    ts, grp = LRU_TILE, LRU_GROUP
    n_lt = xs_ref.shape[0]
    lanes = lambda lt: slice(lt * LANES, (lt + 1) * LANES)
    i = pl.program_id(1)
    nt = pl.num_programs(1)
    cw = cw_ref[...]
    cb = cb_ref[...]

    @pl.when(i == 0)
    def _():
        cf_ref[...] = jnp.zeros_like(cf_ref)
        cbk_ref[...] = jnp.zeros_like(cbk_ref)

    def direction(x_ref, xp_ref, xn_ref, o_ref, c_ref, ti, d):
        first = ti == 0
        last = ti == nt - 1
        for j in range(grp):
            base = j * LRU_XSTRIDE
            for lt in range(n_lt):
                xs_ref[lt, base:base + LRU_HALO, :] = \
                    jnp.where(first, 0.0, xp_ref[j, :, lanes(lt)].astype(F32))
                xs_ref[lt, base + LRU_HALO:base + LRU_HALO + ts, :] = x_ref[j, :, lanes(lt)].astype(F32)
                xs_ref[lt, base + LRU_HALO + ts:base + 2 * LRU_HALO + ts, :] = \
                    jnp.where(last, 0.0, xn_ref[j, :, lanes(lt)].astype(F32))

        def to_time_major(tt, carry):
            r0 = pl.multiple_of(tt * grp, grp)
            src = pl.ds(LRU_HALO - LRU_CONV_LEFT + tt, grp, stride=LRU_XSTRIDE)
            for lt in range(n_lt):
                xt_ref[pl.ds(r0, grp), lanes(lt)] = xs_ref[lt, src, :]
            return carry

        lax.fori_loop(0, ts + CONV_W - 1, to_time_major, 0, unroll=8)

        sp4 = (0.5 * LRU_C) * _softplus_neg(lam_ref[d:d + 1, :])
        ba = 0.5 * ba_ref[d:d + 1, :]
        bx = 0.5 * bx_ref[d:d + 1, :]
        rows_c = LRU_CHUNK * grp

        def gate_chunk(ch, carry):
            r0 = pl.multiple_of(ch * rows_c, rows_c)
            u = cb
            for k in range(CONV_W):
                u = u + cw[k:k + 1, :] * xt_ref[pl.ds(r0 + k * grp, rows_c), :]
            ub = u.astype(BF16)
            tha = jnp.tanh(jnp.dot(ub, wa_ref[d], preferred_element_type=F32) + ba)
            thx = jnp.tanh(jnp.dot(ub, wx_ref[d], preferred_element_type=F32) + bx)
            nla = sp4 * (tha + 1.0)
            a = jnp.exp2(nla * (-LOG2E))
            m = jnp.tanh(nla) * (1.0 + a * a)
            sq = jnp.where(m > 0.0, m * lax.rsqrt(m), 0.0)
            a_ref[pl.ds(r0, rows_c), :] = a
            b_ref[pl.ds(r0, rows_c), :] = sq * ((0.5 * thx + 0.5) * u)
            return carry

        lax.fori_loop(0, ts // LRU_CHUNK, gate_chunk, 0)

        def step(s, h):
            t = (ts - 1 - s) if d == 1 else s
            rows = pl.ds(pl.multiple_of(t * grp, grp), grp)
            h = tuple(a_ref[rows, lanes(lt)] * h[lt] + b_ref[rows, lanes(lt)] for lt in range(n_lt))
            for lt in range(n_lt):
                h_ref[lt, rows, :] = h[lt]
            return h

        h = lax.fori_loop(0, ts, step, tuple(c_ref[:, lanes(lt)] for lt in range(n_lt)), unroll=8)
        for lt in range(n_lt):
            c_ref[:, lanes(lt)] = h[lt]

        pack = 2 * SUBLANES
        for j in range(grp):
            for tb in range(ts // pack):
                for lt in range(n_lt):
                    o_ref[j, tb * pack:(tb + 1) * pack, lanes(lt)] = jnp.concatenate(
                        [h_ref[lt, pl.ds((tb * pack + k * SUBLANES) * grp + j, SUBLANES, stride=grp), :]
                         for k in range(2)], axis=0).astype(o_ref.dtype)

    direction(xf_ref, xfp_ref, xfn_ref, hf_ref, cf_ref, i, 0)
    direction(xb_ref, xbp_ref, xbn_ref, hb_ref, cbk_ref, nt - 1 - i, 1)


def _rg_lru(p_x, conv_w, conv_b, w_a, b_a, w_x, b_x, lam):
    b, s, c = p_x.shape
    ts, grp, halo = LRU_TILE, LRU_GROUP, LRU_HALO
    assert b % grp == 0 and s % ts == 0
    nt = s // ts
    hb_per_tile = ts // halo
    n_halo = s // halo
    cur = lambda rev: pl.BlockSpec(
        (grp, ts, c), (lambda g, i: (g, nt - 1 - i, 0)) if rev else (lambda g, i: (g, i, 0)))
    prev = lambda rev: pl.BlockSpec(
        (grp, halo, c),
        lambda g, i: (g, jnp.maximum(((nt - 1 - i) if rev else i) * hb_per_tile - 1, 0), 0))
    nxt = lambda rev: pl.BlockSpec(
        (grp, halo, c),
        lambda g, i: (g, jnp.minimum((((nt - 1 - i) if rev else i) + 1) * hb_per_tile, n_halo - 1), 0))
    assert grp == SUBLANES and ts % LRU_CHUNK == 0 and c % LANES == 0
    n_lt = c // LANES
    return pl.pallas_call(
        _lru_kernel,
        grid=(b // grp, nt),
        in_specs=[cur(False), prev(False), nxt(False), cur(True), prev(True), nxt(True),
                  _const_spec(conv_w.shape), _const_spec((1, c)),
                  _const_spec(w_a.shape), _const_spec(b_a.shape),
                  _const_spec(w_x.shape), _const_spec(b_x.shape), _const_spec(lam.shape)],
        out_specs=[cur(False), cur(True)],
        out_shape=[jax.ShapeDtypeStruct((b, s, c), BF16)] * 2,
        scratch_shapes=[pltpu.VMEM((n_lt, grp * LRU_XSTRIDE, LANES), F32),
                        pltpu.VMEM(((ts + CONV_W - 1) * grp, c), F32),
                        pltpu.VMEM((ts * grp, c), F32),
                        pltpu.VMEM((ts * grp, c), F32),
                        pltpu.VMEM((n_lt, ts * grp, LANES), F32),
                        pltpu.VMEM((grp, c), F32), pltpu.VMEM((grp, c), F32)],
        compiler_params=_params("parallel", "arbitrary"),
        name="rg_lru",
    )(p_x, p_x, p_x, p_x, p_x, p_x, conv_w, conv_b.reshape(1, c), w_a, b_a, w_x, b_x, lam)


def _block_diag(w):
    h, dh = w.shape[-3], w.shape[-1]
    eye = jnp.eye(h, dtype=w.dtype)
    full = jnp.einsum('...hde,hg->...hdge', w, eye)
    return full.reshape(w.shape[:-3] + (h * dh, h * dh))


def _na_bias(rpb):
    h = rpb.shape[0]
    w, kh, kw = GRID_W, NA_KH, NA_KW
    qc = np.arange(w)
    cs = np.clip(qc - kw // 2, 0, w - kw)
    kc = np.arange(w)
    inwin = (kc[None, :] >= cs[:, None]) & (kc[None, :] < cs[:, None] + kw)
    dc = np.clip(kc[None, :] - qc[:, None] + (NA_KW - 1), 0, 2 * NA_KW - 2)
    var = np.arange(kh)
    ii = np.arange(kh)
    dr = ii[None, :] - var[:, None] + (NA_KH - 1)
    t = rpb[:, :, dc]
    t = jnp.where(jnp.asarray(inwin)[None, None], t * LOG2E, NEG_BIAS)
    t = t[:, dr]
    t = jnp.transpose(t, (1, 0, 3, 2, 4)).reshape(kh, h // 2, 2 * w, kh * w)
    return t.astype(F32)


def _na_kernel(q_ref, k_ref, v_ref, bias_ref, o_ref, *, rows, n_pairs):
    w = GRID_W
    band = NA_KH * w
    lane = lax.broadcasted_iota(jnp.int32, (w, LANES), 1)
    low = lane < HEAD_DIM

    def body(it, carry):
        chains = []
        for rr in range(NA_ROWS_PER_STEP):
            r = it * NA_ROWS_PER_STEP + rr
            rs = jnp.clip(r - NA_KH // 2, 0, rows - NA_KH)
            q0 = pl.multiple_of(r * w, w)
            k0 = pl.multiple_of(rs * w, w)
            for pr in range(n_pairs):
                chains.append((r - rs, q0, k0, pr, slice(pr * LANES, (pr + 1) * LANES)))
        scores = []
        for var, q0, k0, pr, sl in chains:
            q2 = q_ref[0, pl.ds(q0, w), sl]
            zero = jnp.zeros_like(q2)
            qq = jnp.concatenate([jnp.where(low, q2, zero), jnp.where(low, zero, q2)], axis=0)
            kk = k_ref[0, pl.ds(k0, band), sl]
            s = lax.dot_general(qq, kk, (((1,), (1,)), ((), ())), preferred_element_type=F32)
            scores.append(s + bias_ref[var, pr])
        probs = []
        for s in scores:
            e = jnp.exp2(s - jnp.max(s, axis=-1, keepdims=True))
            probs.append((e.astype(BF16), jnp.sum(e, axis=-1, keepdims=True)))
        for (var, q0, k0, pr, sl), (e, l) in zip(chains, probs):
            vv = v_ref[0, pl.ds(k0, band), sl]
            pv = jnp.dot(e, vv, preferred_element_type=F32) / l
            o = jnp.where(low, pv[0:w], pv[w:2 * w])
            o_ref[0, pl.ds(q0, w), sl] = o.astype(o_ref.dtype)
        return carry

    lax.fori_loop(0, rows // NA_ROWS_PER_STEP, body, 0)


def _neighbourhood_attention(q, k, v, bias):
    b, s, c = q.shape
    rows = s // GRID_W
    assert rows >= NA_KH and rows % NA_ROWS_PER_STEP == 0 and c % LANES == 0
    n_pairs = c // LANES
    seq = pl.BlockSpec((1, s, c), lambda i: (i, 0, 0))
    return pl.pallas_call(
        functools.partial(_na_kernel, rows=rows, n_pairs=n_pairs),
        grid=(b,),
        in_specs=[seq, seq, seq, _const_spec(bias.shape)],
        out_specs=seq,
        out_shape=jax.ShapeDtypeStruct((b, s, c), BF16),
        compiler_params=_params("parallel"),
        name="neighbourhood_attn",
    )(q, k, v, bias)


def _tail_kernel(yf_ref, hf_ref, hb_ref, pg_ref, yn_ref, x_ref, mod_ref, wo_ref,
                 g_ref, wg_ref, wu_ref, wd_ref, *rest, d_ff, splits):
    if splits is None:
        gf_ref, o_ref, acc_ref = rest
    else:
        modn_ref, gn_ref, win_ref, o_ref = rest[:4]
        p_refs, acc_ref = rest[4:-1], rest[-1]
    tm = x_ref.shape[1]
    n_lt = yf_ref.shape[1]

    yl = (_gelu_tanh(pg_ref[0].astype(F32))
          * (hf_ref[0].astype(F32) + hb_ref[0].astype(F32))).astype(BF16)
    yf = [jnp.concatenate([yf_ref[0, lt, a * ROW_PITCH:a * ROW_PITCH + GRID_W, :]
                           for a in range(tm // GRID_W)], axis=0).astype(BF16) for lt in range(n_lt)]
    mixed = jnp.concatenate(yf + [yl, yn_ref[0]], axis=1)
    y = jnp.dot(mixed, wo_ref[...], preferred_element_type=F32)
    o_ref[0] = x_ref[0] + mod_ref[0, 2:3, :] * y

    h = _norm_mod(o_ref[0], g_ref[...], mod_ref[0, 3:4, :], mod_ref[0, 4:5, :]).astype(BF16)
    for j in range(d_ff // FF_CHUNK):
        cs = slice(j * FF_CHUNK, (j + 1) * FF_CHUNK)
        gt = jnp.dot(h, wg_ref[:, cs], preferred_element_type=F32)
        up = jnp.dot(h, wu_ref[:, cs], preferred_element_type=F32)
        act = (gt * jax.nn.sigmoid(gt) * up).astype(BF16)
        part = jnp.dot(act, wd_ref[cs, :], preferred_element_type=F32)
        if j == 0:
            acc_ref[...] = part
        else:
            acc_ref[...] += part
    x = o_ref[0] + mod_ref[0, 5:6, :] * acc_ref[...]
    if splits is None:
        ms = jnp.mean(x * x, axis=-1, keepdims=True)
        o_ref[0] = x * lax.rsqrt(ms + EPS) * gf_ref[...]
    else:
        o_ref[0] = x
        hn = _norm_mod(x, gn_ref[...], modn_ref[0, 0:1, :], modn_ref[0, 1:2, :]).astype(BF16)
        _emit_in_proj(hn, win_ref, p_refs, splits)


def _layer_tail(y_f, h_f, h_b, p_g, y_n, x, mod, w_out, g_ffn, w_gate, w_up, w_down,
                nxt=None, g_final=None, splits=None):
    b, s, d = x.shape
    tm = TOKEN_TILE
    d_ff = w_gate.shape[-1]
    assert d_ff % FF_CHUNK == 0 and tm % GRID_W == 0
    tok = lambda w: pl.BlockSpec((1, tm, w), lambda i, j: (i, j, 0))
    mod_spec = pl.BlockSpec((1, N_MOD, d), lambda i, j: (i, 0, 0))
    in_specs = [pl.BlockSpec((1, y_f.shape[1], tm // GRID_W * ROW_PITCH, LANES), lambda i, j: (i, 0, j, 0)),
                tok(h_f.shape[-1]), tok(h_b.shape[-1]), tok(p_g.shape[-1]), tok(y_n.shape[-1]), tok(d),
                mod_spec, _const_spec(w_out.shape), _const_spec((1, d)),
                _const_spec(w_gate.shape), _const_spec(w_up.shape), _const_spec(w_down.shape)]
    args = [y_f, h_f, h_b, p_g, y_n, x, mod, w_out, g_ffn, w_gate, w_up, w_down]
    out_shape = [jax.ShapeDtypeStruct((b, s, d), F32)]
    out_specs = [tok(d)]
    if nxt is None:
        in_specs += [_const_spec((1, d))]
        args += [g_final]
        splits = None
    else:
        mod_n, g_n, w_in_n = nxt
        in_specs += [mod_spec, _const_spec((1, d)), _const_spec(w_in_n.shape)]
        args += [mod_n, g_n, w_in_n]
        shapes, specs = _in_proj_out(b, s, tm, splits)
        out_shape += shapes
        out_specs += specs
    return pl.pallas_call(
        functools.partial(_tail_kernel, d_ff=d_ff, splits=splits),
        grid=(b, s // tm),
        in_specs=in_specs,
        out_specs=out_specs,
        out_shape=out_shape,
        scratch_shapes=[pltpu.VMEM((tm, d), F32)],
        compiler_params=pltpu.CompilerParams(dimension_semantics=("parallel", "arbitrary"),
                                             vmem_limit_bytes=TAIL_VMEM_LIMIT),
        name="layer_tail",
    )(*args)


def kernel(x, c, w_ada, b_ada, g_mix, g_ffn, w_in, w_fourier, conv_w, conv_b, lru_w_a, lru_b_a,
           lru_w_x, lru_b_x, lru_lambda, na_rpb, w_out, w_ffn_gate, w_ffn_up, w_ffn_down, g_final):
    b, s, d = x.shape
    depth = w_in.shape[0]
    n_groups = w_fourier.shape[1]
    d_f = n_groups * HEAD_DIM
    d_l = conv_w.shape[-1]
    d_n = na_rpb.shape[1] * HEAD_DIM

    mod = _modulation(c, w_ada, b_ada).reshape(depth, b, N_MOD, d)
    tables = _fourier_tables(n_groups)
    splits = _in_proj_splits(d_f, d_l, d_n)
    w_in_b = w_in.astype(BF16)

    p_f, p_x, p_g, q, k, v = _in_proj(x, mod[0], g_mix[0].reshape(1, d), w_in_b[0], splits)
    for l in range(depth):
        y_f = _fourier_mix(p_f, _block_diag(w_fourier[l]).astype(BF16), tables)
        h_f, h_b = _rg_lru(p_x, conv_w[l], conv_b[l],
                           _block_diag(0.5 * lru_w_a[l]).astype(BF16), lru_b_a[l],
                           _block_diag(0.5 * lru_w_x[l]).astype(BF16), lru_b_x[l], lru_lambda[l])
        y_n = _neighbourhood_attention(q, k, v, _na_bias(na_rpb[l]))
        common = (y_f, h_f, h_b, p_g, y_n, x, mod[l], w_out[l].astype(BF16), g_ffn[l].reshape(1, d),
                  w_ffn_gate[l].astype(BF16), w_ffn_up[l].astype(BF16), w_ffn_down[l].astype(BF16))
        if l + 1 < depth:
            x, p_f, p_x, p_g, q, k, v = _layer_tail(
                *common, nxt=(mod[l + 1], g_mix[l + 1].reshape(1, d), w_in_b[l + 1]), splits=splits)
        else:
            (x,) = _layer_tail(*common, g_final=g_final.reshape(1, d))
    return x
```

```python
import functools

import numpy as np
import jax
import jax.numpy as jnp
from jax import lax
from jax.experimental import pallas as pl
from jax.experimental.pallas import tpu as pltpu

F32 = jnp.float32
BF16 = jnp.bfloat16

HEAD_DIM = 64
GRID_W = 64
CONV_W = 4
LRU_C = 8.0
NA_KH = 8
NA_KW = 16
N_MOD = 6
EPS = 1e-6
NEG_BIAS = -1e30
LOG2E = 1.4426950408889634
NA_ROWS_PER_STEP = 2

LANES = 128
SUBLANES = 8
VMEM_LIMIT = 56 * 1024 * 1024
TAIL_VMEM_LIMIT = 60 * 1024 * 1024

TOKEN_TILE = 512
FF_CHUNK = 256
LRU_TILE = 128
LRU_GROUP = 8


def _params(*sem):
    return pltpu.CompilerParams(dimension_semantics=sem, vmem_limit_bytes=VMEM_LIMIT)


def _const_spec(shape):
    nd = len(shape)
    return pl.BlockSpec(shape, lambda *_: (0,) * nd, pipeline_mode=pl.Buffered(1))


def _layer_spec(arr, l):
    nd = arr.ndim - 1
    return pl.BlockSpec((None,) + arr.shape[1:], lambda *_: (l,) + (0,) * nd,
                        pipeline_mode=pl.Buffered(1))


def _mod_spec(mod, l):
    return pl.BlockSpec((None, 1) + mod.shape[2:], lambda i, j: (l, i, 0, 0))


def _norm_mod(x, g, shift, scale):
    ms = jnp.mean(x * x, axis=-1, keepdims=True)
    y = x * lax.rsqrt(ms + EPS) * g
    return y * (1.0 + scale) + shift


def _gelu_tanh(x):
    return 0.5 * x * (1.0 + jnp.tanh(0.7978845608028654 * (x + 0.044715 * (x * x * x))))


def _mod_kernel(c_ref, w_ref, b_ref, o_ref):
    c = c_ref[...]
    ca = c * jax.nn.sigmoid(c)
    o_ref[0] = jnp.dot(ca, w_ref[0], precision=lax.Precision.HIGHEST,
                       preferred_element_type=F32) + b_ref[0]


def _modulation(c, w_ada, b_ada):
    depth, d, n = w_ada.shape
    b = c.shape[0]
    tn = 1024
    return pl.pallas_call(
        _mod_kernel,
        grid=(depth, n // tn),
        in_specs=[pl.BlockSpec((b, d), lambda l, j: (0, 0)),
                  pl.BlockSpec((1, d, tn), lambda l, j: (l, 0, j)),
                  pl.BlockSpec((1, 1, tn), lambda l, j: (l, 0, j))],
        out_specs=pl.BlockSpec((1, b, tn), lambda l, j: (l, 0, j)),
        out_shape=jax.ShapeDtypeStruct((depth, b, n), F32),
        compiler_params=_params("arbitrary", "arbitrary"),
        name="adaln_mod",
    )(c, w_ada, b_ada.reshape(depth, 1, n))


ROW_PITCH = GRID_W + SUBLANES


def _in_proj_splits(d_f, d_l, d_n):
    o1 = d_f
    o2 = o1 + d_l
    o3 = o2 + d_l
    o4 = o3 + d_n
    o5 = o4 + d_n
    o6 = o5 + d_n
    return ((0, o1, 1.0), (o1, o2, 1.0), (o2, o3, 1.0),
            (o3, o4, HEAD_DIM ** -0.5 * LOG2E), (o4, o5, 1.0), (o5, o6, 1.0))


def _in_proj_out(b, s, tm, splits):
    (lo, hi, _), rest = splits[0], splits[1:]
    rows = s // GRID_W
    shapes = [jax.ShapeDtypeStruct((b, (hi - lo) // LANES, rows * ROW_PITCH, LANES), F32)]
    specs = [pl.BlockSpec((1, (hi - lo) // LANES, tm // GRID_W * ROW_PITCH, LANES),
                          lambda i, j: (i, 0, j, 0))]
    for lo, hi, _ in rest:
        shapes.append(jax.ShapeDtypeStruct((b, s, hi - lo), BF16))
        specs.append(pl.BlockSpec((1, tm, hi - lo), lambda i, j: (i, j, 0)))
    return shapes, specs


IN_PROJ_GROUPS = ((1, 2), (3, 4), (5,))


def _emit_in_proj(h, w_ref, out_refs, splits):
    pf_ref = out_refs[0]
    lo, hi, _ = splits[0]
    p = jnp.dot(h, w_ref[:, lo:hi], preferred_element_type=F32)
    pad = jnp.zeros((ROW_PITCH - GRID_W, LANES), F32)
    for lt in range((hi - lo) // LANES):
        for a in range(h.shape[0] // GRID_W):
            pf_ref[0, lt, a * ROW_PITCH:a * ROW_PITCH + GRID_W, :] = \
                p[a * GRID_W:(a + 1) * GRID_W, lt * LANES:(lt + 1) * LANES]
            pf_ref[0, lt, a * ROW_PITCH + GRID_W:(a + 1) * ROW_PITCH, :] = pad
    for group in IN_PROJ_GROUPS:
        g_lo, g_hi = splits[group[0]][0], splits[group[-1]][1]
        p = jnp.dot(h, w_ref[:, g_lo:g_hi], preferred_element_type=F32)
        for idx in group:
            lo, hi, scale = splits[idx]
            part = p[:, lo - g_lo:hi - g_lo]
            if scale != 1.0:
                part = part * scale
            out_refs[idx][0] = part.astype(out_refs[idx].dtype)


def _in_proj_kernel(x_ref, mod_ref, g_ref, w_ref, *out_refs, splits):
    h = _norm_mod(x_ref[0], g_ref[...], mod_ref[0, 0:1, :], mod_ref[0, 1:2, :]).astype(BF16)
    _emit_in_proj(h, w_ref, out_refs, splits)


def _in_proj(x, mod, g_mix, w_in, l, splits):
    b, s, d = x.shape
    tm = TOKEN_TILE
    shapes, specs = _in_proj_out(b, s, tm, splits)
    return pl.pallas_call(
        functools.partial(_in_proj_kernel, splits=splits),
        grid=(b, s // tm),
        in_specs=[pl.BlockSpec((1, tm, d), lambda i, j: (i, j, 0)),
                  _mod_spec(mod, l), _layer_spec(g_mix, l), _layer_spec(w_in, l)],
        out_specs=specs,
        out_shape=shapes,
        compiler_params=_params("parallel", "arbitrary"),
        name="in_proj",
    )(x, mod, g_mix, w_in)


FOURIER_SLAB = 2 * GRID_W + SUBLANES
FOURIER_COLS = 8


def _fourier_tables(n_groups):
    n = GRID_W
    idx = np.arange(n)
    ang1 = 2.0 * np.pi * np.outer(idx, idx) / n
    f1 = np.concatenate([np.cos(ang1), np.sin(ang1)], axis=0)
    c = idx[:, None, None]
    dd = idx[None, :, None]
    bb = idx[None, None, :]
    ang2 = 2.0 * np.pi * bb * (n * dd + c) / (n * n)
    gc, gs = np.cos(ang2), np.sin(ang2)
    g2 = np.concatenate([np.concatenate([gc, -gs], axis=2),
                         np.concatenate([gs, gc], axis=2)], axis=1)
    angd = 2.0 * np.pi * np.outer(np.arange(HEAD_DIM), np.arange(HEAD_DIM)) / HEAD_DIM
    eye = np.eye(n_groups)
    cs = np.concatenate([np.kron(eye, np.cos(angd)), -np.kron(eye, np.sin(angd))], axis=0)
    return jnp.asarray(f1, BF16), jnp.asarray(g2, BF16), jnp.asarray(cs, BF16)


def _fourier_kernel(x_ref, f1_ref, g2_ref, cs_ref, w_ref, o_ref, z_ref, y_ref, *, width, scale):
    n = GRID_W
    n_lt = width // LANES
    f1 = f1_ref[...]
    for b in range(n):
        xb = jnp.concatenate([x_ref[0, lt, pl.ds(b, n, stride=ROW_PITCH), :] for lt in range(n_lt)],
                             axis=1).astype(BF16)
        slab = jnp.dot(f1, xb, preferred_element_type=F32)
        for lt in range(n_lt):
            z_ref[lt, b * FOURIER_SLAB:b * FOURIER_SLAB + 2 * n, :] = slab[:, lt * LANES:(lt + 1) * LANES]
    for c in range(n):
        z = jnp.concatenate(
            [jnp.concatenate([z_ref[lt, pl.ds(off + c, n, stride=FOURIER_SLAB), :]
                              for lt in range(n_lt)], axis=1)
             for off in (0, n)], axis=0).astype(BF16)
        y = jnp.dot(g2_ref[c], z, preferred_element_type=F32)
        y_ref[c * n:(c + 1) * n, 0:width] = y[0:n].astype(BF16)
        y_ref[c * n:(c + 1) * n, width:2 * width] = y[n:2 * n].astype(BF16)
    cs = cs_ref[...]
    w = w_ref[...]
    for blk in range(n // FOURIER_COLS):
        rows = slice(blk * FOURIER_COLS * n, (blk + 1) * FOURIER_COLS * n)
        f = jnp.dot(y_ref[rows, :], cs, preferred_element_type=F32) * scale
        out = jnp.dot(f.astype(BF16), w, preferred_element_type=F32)
        for cc in range(FOURIER_COLS):
            c = blk * FOURIER_COLS + cc
            for lt in range(n_lt):
                o_ref[0, lt, pl.ds(c, n, stride=ROW_PITCH), :] = \
                    out[cc * n:(cc + 1) * n, lt * LANES:(lt + 1) * LANES]
    pad = jnp.zeros((ROW_PITCH - n, LANES), F32)
    for lt in range(n_lt):
        for dd in range(n):
            o_ref[0, lt, dd * ROW_PITCH + n:(dd + 1) * ROW_PITCH, :] = pad


def _fourier_mix(p_f, w_blk, l, tables):
    b, n_lt, prow, _ = p_f.shape
    n = GRID_W
    assert prow == n * ROW_PITCH, "sequence DFT is factored as 64 x 64"
    width = n_lt * LANES
    f1, g2, cs = tables
    scale = float(1.0 / np.sqrt(n * n * HEAD_DIM))
    blk = pl.BlockSpec((1, n_lt, prow, LANES), lambda i: (i, 0, 0, 0))
    return pl.pallas_call(
        functools.partial(_fourier_kernel, width=width, scale=scale),
        grid=(b,),
        in_specs=[blk,
                  _const_spec(f1.shape), _const_spec(g2.shape),
                  _const_spec(cs.shape), _layer_spec(w_blk, l)],
        out_specs=blk,
        out_shape=jax.ShapeDtypeStruct(p_f.shape, F32),
        scratch_shapes=[pltpu.VMEM((n_lt, n * FOURIER_SLAB, LANES), F32),
                        pltpu.VMEM((n * n, 2 * width), BF16)],
        compiler_params=_params("parallel"),
        name="fourier_mix",
    )(p_f, f1, g2, cs, w_blk)


LRU_HALO = 2 * SUBLANES
LRU_XSTRIDE = LRU_TILE + 2 * LRU_HALO + SUBLANES
LRU_CONV_LEFT = 2
LRU_CHUNK = 32


def _softplus_neg(lam):
    e = jnp.exp(-jnp.abs(lam))
    u = 1.0 + e
    l1p = jnp.where(u == 1.0, e, jnp.log(u) * (e / (u - 1.0)))
    return jnp.maximum(-lam, 0.0) + l1p


def _lru_kernel(x_ref, xp_ref, xn_ref, cw_ref, cb_ref, w_ref, ba_ref, bx_ref, lam_ref,
                s_ref, p_ref, carry_ref,
                xs_ref, xt_ref, af_ref, bf_ref, ab_ref, bb_ref, hs_ref, ps_ref,
                cf_ref, lsum_ref, psum_ref, *, nt):
    ts, grp = LRU_TILE, LRU_GROUP
    n_lt = xs_ref.shape[0]
    c = n_lt * LANES
    lanes = lambda lt: slice(lt * LANES, (lt + 1) * LANES)
    i = pl.program_id(1)
    first = i == 0
    last = i == nt - 1

    @pl.when(first)
    def _():
        cf_ref[...] = jnp.zeros_like(cf_ref)

    for j in range(grp):
        base = j * LRU_XSTRIDE
        for lt in range(n_lt):
            xs_ref[lt, base:base + LRU_HALO, :] = \
                jnp.where(first, 0.0, xp_ref[j, :, lanes(lt)].astype(F32))
            xs_ref[lt, base + LRU_HALO:base + LRU_HALO + ts, :] = x_ref[j, :, lanes(lt)].astype(F32)
            xs_ref[lt, base + LRU_HALO + ts:base + 2 * LRU_HALO + ts, :] = \
                jnp.where(last, 0.0, xn_ref[j, :, lanes(lt)].astype(F32))

    def to_time_major(tt, carry):
        r0 = pl.multiple_of(tt * grp, grp)
        src = pl.ds(LRU_HALO - LRU_CONV_LEFT + tt, grp, stride=LRU_XSTRIDE)
        for lt in range(n_lt):
            xt_ref[pl.ds(r0, grp), lanes(lt)] = xs_ref[lt, src, :]
        return carry

    lax.fori_loop(0, ts + CONV_W - 1, to_time_major, 0, unroll=8)

    cw = cw_ref[...]
    cb = cb_ref[...]
    coef = [(0.5 * LRU_C) * _softplus_neg(lam_ref[d:d + 1, :]) for d in range(2)]
    ba = [0.5 * ba_ref[d:d + 1, :] for d in range(2)]
    bx = [0.5 * bx_ref[d:d + 1, :] for d in range(2)]
    rows_c = LRU_CHUNK * grp

    def gate_chunk(ch, carry):
        r0 = pl.multiple_of(ch * rows_c, rows_c)
        u = cb
        for k in range(CONV_W):
            u = u + cw[k:k + 1, :] * xt_ref[pl.ds(r0 + k * grp, rows_c), :]
        ub = u.astype(BF16)
        for d, (a_ref, b_ref) in enumerate(((af_ref, bf_ref), (ab_ref, bb_ref))):
            z = jnp.dot(ub, w_ref[d], preferred_element_type=F32)
            tha = jnp.tanh(z[:, 0:c] + ba[d])
            thx = jnp.tanh(z[:, c:2 * c] + bx[d])
            nla = coef[d] * (tha + 1.0)
            a = jnp.exp2(nla * (-LOG2E))
            m = jnp.tanh(nla) * (1.0 + a * a)
            sq = jnp.where(m > 0.0, m * lax.rsqrt(m), 0.0)
            a_ref[pl.ds(r0, rows_c), :] = a
            b_ref[pl.ds(r0, rows_c), :] = sq * ((0.5 * thx + 0.5) * u)
        return carry

    lax.fori_loop(0, ts // LRU_CHUNK, gate_chunk, 0)

    def bwd_step(s, carry):
        h, p = carry
        rows = pl.ds(pl.multiple_of((ts - 1 - s) * grp, grp), grp)
        a = [ab_ref[rows, lanes(lt)] for lt in range(n_lt)]
        h = tuple(a[lt] * h[lt] + bb_ref[rows, lanes(lt)] for lt in range(n_lt))
        p = tuple(a[lt] * p[lt] for lt in range(n_lt))
        for lt in range(n_lt):
            hs_ref[lt, rows, :] = h[lt]
            ps_ref[lt, rows, :] = p[lt]
        return h, p

    zero = jnp.zeros((grp, LANES), F32)
    h, p = lax.fori_loop(0, ts, bwd_step, ((zero,) * n_lt, (zero + 1.0,) * n_lt), unroll=8)
    for lt in range(n_lt):
        lsum_ref[i, :, lanes(lt)] = h[lt]
        psum_ref[i, :, lanes(lt)] = p[lt]

    def fwd_step(t, h):
        rows = pl.ds(pl.multiple_of(t * grp, grp), grp)
        h = tuple(af_ref[rows, lanes(lt)] * h[lt] + bf_ref[rows, lanes(lt)] for lt in range(n_lt))
        for lt in range(n_lt):
            hs_ref[lt, rows, :] = hs_ref[lt, rows, :] + h[lt]
        return h

    h = lax.fori_loop(0, ts, fwd_step, tuple(cf_ref[:, lanes(lt)] for lt in range(n_lt)), unroll=8)
    for lt in range(n_lt):
        cf_ref[:, lanes(lt)] = h[lt]

    pack = 2 * SUBLANES
    for src_ref, o_ref in ((hs_ref, s_ref), (ps_ref, p_ref)):
        for j in range(grp):
            for tb in range(ts // pack):
                for lt in range(n_lt):
                    o_ref[j, tb * pack:(tb + 1) * pack, lanes(lt)] = jnp.concatenate(
                        [src_ref[lt, pl.ds((tb * pack + k * SUBLANES) * grp + j, SUBLANES, stride=grp), :]
                         for k in range(2)], axis=0).astype(o_ref.dtype)

    @pl.when(last)
    def _():
        hn = jnp.zeros((grp, c), F32)
        for t in reversed(range(nt)):
            carry_ref[t] = hn
            hn = lsum_ref[t] + psum_ref[t] * hn


def _rg_lru(p_x, conv_w, conv_b, w_gates, b_a, b_x, lam, l):
    b, s, c = p_x.shape
    ts, grp, halo = LRU_TILE, LRU_GROUP, LRU_HALO
    assert b % grp == 0 and s % ts == 0
    assert grp == SUBLANES and ts % LRU_CHUNK == 0 and c % LANES == 0
    nt = s // ts
    hb_per_tile = ts // halo
    n_halo = s // halo
    n_lt = c // LANES
    cur = pl.BlockSpec((grp, ts, c), lambda g, i: (g, i, 0))
    prev = pl.BlockSpec((grp, halo, c), lambda g, i: (g, jnp.maximum(i * hb_per_tile - 1, 0), 0))
    nxt = pl.BlockSpec((grp, halo, c),
                       lambda g, i: (g, jnp.minimum((i + 1) * hb_per_tile, n_halo - 1), 0))
    tile_f32 = pltpu.VMEM((ts * grp, c), F32)
    planes_f32 = pltpu.VMEM((n_lt, ts * grp, LANES), F32)
    return pl.pallas_call(
        functools.partial(_lru_kernel, nt=nt),
        grid=(b // grp, nt),
        in_specs=[cur, prev, nxt,
                  _layer_spec(conv_w, l), _layer_spec(conv_b, l), _layer_spec(w_gates, l),
                  _layer_spec(b_a, l), _layer_spec(b_x, l), _layer_spec(lam, l)],
        out_specs=[cur, cur, pl.BlockSpec((nt, grp, c), lambda g, i: (0, g, 0))],
        out_shape=[jax.ShapeDtypeStruct((b, s, c), BF16), jax.ShapeDtypeStruct((b, s, c), BF16),
                   jax.ShapeDtypeStruct((nt, b, c), F32)],
        scratch_shapes=[pltpu.VMEM((n_lt, grp * LRU_XSTRIDE, LANES), F32),
                        pltpu.VMEM(((ts + CONV_W - 1) * grp, c), F32),
                        tile_f32, tile_f32, tile_f32, tile_f32,
                        planes_f32, planes_f32,
                        pltpu.VMEM((grp, c), F32),
                        pltpu.VMEM((nt, grp, c), F32),
                        pltpu.VMEM((nt, grp, c), F32)],
        compiler_params=_params("parallel", "arbitrary"),
        name="rg_lru",
    )(p_x, p_x, p_x, conv_w, conv_b, w_gates, b_a, b_x, lam)


def _block_diag(w):
    h, dh = w.shape[-3], w.shape[-1]
    lead = [(0, 0)] * (w.ndim - 2)
    rows = [jnp.pad(w[..., k, :, :], lead + [(k * dh, (h - 1 - k) * dh)]) for k in range(h)]
    return jnp.concatenate(rows, axis=-2)


def _na_bias(rpb):
    n_l, h = rpb.shape[:2]
    w, kh, kw = GRID_W, NA_KH, NA_KW
    qc = np.arange(w)[:, None]
    kc = np.arange(w)[None, :]
    cs = np.clip(qc - kw // 2, 0, w - kw)
    inwin = (kc >= cs) & (kc < cs + kw)
    sel_c = (inwin[:, :, None]
             & (np.arange(2 * kw - 1)[None, None, :] == (kc - qc + kw - 1)[:, :, None])).astype(np.float32)
    var = np.arange(kh)[:, None]
    ii = np.arange(kh)[None, :]
    sel_r = (np.arange(2 * kh - 1)[None, None, :] == (ii - var + kh - 1)[:, :, None]).astype(np.float32)
    t = jnp.einsum('lhrc,vir,qkc->lvhqik', rpb * LOG2E, sel_r, sel_c,
                   precision=lax.Precision.HIGHEST)
    t = t + np.where(inwin, 0.0, NEG_BIAS).astype(np.float32)[:, None, :]
    return t.reshape(n_l, kh, h // 2, 2 * w, kh * w)


def _na_kernel(q_ref, k_ref, v_ref, bias_ref, o_ref, *, rows, n_pairs):
    w = GRID_W
    band = NA_KH * w
    lane = lax.broadcasted_iota(jnp.int32, (w, LANES), 1)
    low = lane < HEAD_DIM

    def body(it, carry):
        chains = []
        for rr in range(NA_ROWS_PER_STEP):
            r = it * NA_ROWS_PER_STEP + rr
            rs = jnp.clip(r - NA_KH // 2, 0, rows - NA_KH)
            q0 = pl.multiple_of(r * w, w)
            k0 = pl.multiple_of(rs * w, w)
            for pr in range(n_pairs):
                chains.append((r - rs, q0, k0, pr, slice(pr * LANES, (pr + 1) * LANES)))
        scores = []
        for var, q0, k0, pr, sl in chains:
            q2 = q_ref[0, pl.ds(q0, w), sl]
            zero = jnp.zeros_like(q2)
            qq = jnp.concatenate([jnp.where(low, q2, zero), jnp.where(low, zero, q2)], axis=0)
            kk = k_ref[0, pl.ds(k0, band), sl]
            s = lax.dot_general(qq, kk, (((1,), (1,)), ((), ())), preferred_element_type=F32)
            scores.append(s + bias_ref[var, pr])
        probs = []
        for s in scores:
            e = jnp.exp2(s - jnp.max(s, axis=-1, keepdims=True))
            probs.append((e.astype(BF16), jnp.sum(e, axis=-1, keepdims=True)))
        for (var, q0, k0, pr, sl), (e, l) in zip(chains, probs):
            vv = v_ref[0, pl.ds(k0, band), sl]
            pv = jnp.dot(e, vv, preferred_element_type=F32) / l
            o = jnp.where(low, pv[0:w], pv[w:2 * w])
            o_ref[0, pl.ds(q0, w), sl] = o.astype(o_ref.dtype)
        return carry

    lax.fori_loop(0, rows // NA_ROWS_PER_STEP, body, 0)


def _neighbourhood_attention(q, k, v, bias, l):
    b, s, c = q.shape
    rows = s // GRID_W
    assert rows >= NA_KH and rows % NA_ROWS_PER_STEP == 0 and c % LANES == 0
    n_pairs = c // LANES
    seq = pl.BlockSpec((1, s, c), lambda i: (i, 0, 0))
    return pl.pallas_call(
        functools.partial(_na_kernel, rows=rows, n_pairs=n_pairs),
        grid=(b,),
        in_specs=[seq, seq, seq, _layer_spec(bias, l)],
        out_specs=seq,
        out_shape=jax.ShapeDtypeStruct((b, s, c), BF16),
        compiler_params=_params("parallel"),
        name="neighbourhood_attn",
    )(q, k, v, bias)


def _tail_kernel(yf_ref, s_ref, p_ref, hn_ref, pg_ref, yn_ref, x_ref, mod_ref, wo_ref,
                 g_ref, wg_ref, wu_ref, wd_ref, *rest, d_ff, splits):
    if splits is None:
        gf_ref, o_ref, acc_ref = rest
    else:
        modn_ref, gn_ref, win_ref, o_ref = rest[:4]
        p_refs, acc_ref = rest[4:-1], rest[-1]
    tm = x_ref.shape[1]
    n_lt = yf_ref.shape[1]
    ts = LRU_TILE

    h_bwd_tail = jnp.concatenate(
        [p_ref[0, k * ts:(k + 1) * ts, :].astype(F32) * hn_ref[0, 0, k:k + 1, :]
         for k in range(tm // ts)], axis=0)
    yl = (_gelu_tanh(pg_ref[0].astype(F32)) * (s_ref[0].astype(F32) + h_bwd_tail)).astype(BF16)
    yf = [jnp.concatenate([yf_ref[0, lt, a * ROW_PITCH:a * ROW_PITCH + GRID_W, :]
                           for a in range(tm // GRID_W)], axis=0).astype(BF16) for lt in range(n_lt)]
    mixed = jnp.concatenate(yf + [yl, yn_ref[0]], axis=1)
    y = jnp.dot(mixed, wo_ref[...], preferred_element_type=F32)
    o_ref[0] = x_ref[0] + mod_ref[0, 2:3, :] * y

    h = _norm_mod(o_ref[0], g_ref[...], mod_ref[0, 3:4, :], mod_ref[0, 4:5, :]).astype(BF16)
    for j in range(d_ff // FF_CHUNK):
        cs = slice(j * FF_CHUNK, (j + 1) * FF_CHUNK)
        gt = jnp.dot(h, wg_ref[:, cs], preferred_element_type=F32)
        up = jnp.dot(h, wu_ref[:, cs], preferred_element_type=F32)
        act = (gt * jax.nn.sigmoid(gt) * up).astype(BF16)
        part = jnp.dot(act, wd_ref[cs, :], preferred_element_type=F32)
        if j == 0:
            acc_ref[...] = part
        else:
            acc_ref[...] += part
    x = o_ref[0] + mod_ref[0, 5:6, :] * acc_ref[...]
    if splits is None:
        ms = jnp.mean(x * x, axis=-1, keepdims=True)
        o_ref[0] = x * lax.rsqrt(ms + EPS) * gf_ref[...]
    else:
        o_ref[0] = x
        hn = _norm_mod(x, gn_ref[...], modn_ref[0, 0:1, :], modn_ref[0, 1:2, :]).astype(BF16)
        _emit_in_proj(hn, win_ref, p_refs, splits)


def _layer_tail(y_f, s_lru, p_lru, h_next, p_g, y_n, x, mod, w_out, g_ffn, w_gate, w_up, w_down, l,
                nxt=None, g_final=None, splits=None):
    b, s, d = x.shape
    tm = TOKEN_TILE
    d_ff = w_gate.shape[-1]
    assert d_ff % FF_CHUNK == 0 and tm % GRID_W == 0 and tm % LRU_TILE == 0
    tok = lambda w: pl.BlockSpec((1, tm, w), lambda i, j: (i, j, 0))
    in_specs = [pl.BlockSpec((1, y_f.shape[1], tm // GRID_W * ROW_PITCH, LANES), lambda i, j: (i, 0, j, 0)),
                tok(s_lru.shape[-1]), tok(p_lru.shape[-1]),
                pl.BlockSpec((1, 1) + h_next.shape[2:], lambda i, j: (i, j, 0, 0)),
                tok(p_g.shape[-1]), tok(y_n.shape[-1]), tok(d),
                _mod_spec(mod, l), _layer_spec(w_out, l), _layer_spec(g_ffn, l),
                _layer_spec(w_gate, l), _layer_spec(w_up, l), _layer_spec(w_down, l)]
    args = [y_f, s_lru, p_lru, h_next, p_g, y_n, x, mod, w_out, g_ffn, w_gate, w_up, w_down]
    out_shape = [jax.ShapeDtypeStruct((b, s, d), F32)]
    out_specs = [tok(d)]
    if nxt is None:
        in_specs += [_const_spec(g_final.shape)]
        args += [g_final]
        splits = None
    else:
        g_mix, w_in = nxt
        in_specs += [_mod_spec(mod, l + 1), _layer_spec(g_mix, l + 1), _layer_spec(w_in, l + 1)]
        args += [mod, g_mix, w_in]
        shapes, specs = _in_proj_out(b, s, tm, splits)
        out_shape += shapes
        out_specs += specs
    return pl.pallas_call(
        functools.partial(_tail_kernel, d_ff=d_ff, splits=splits),
        grid=(b, s // tm),
        in_specs=in_specs,
        out_specs=out_specs,
        out_shape=out_shape,
        scratch_shapes=[pltpu.VMEM((tm, d), F32)],
        compiler_params=pltpu.CompilerParams(dimension_semantics=("parallel", "arbitrary"),
                                             vmem_limit_bytes=TAIL_VMEM_LIMIT),
        name="layer_tail",
    )(*args)


def kernel(x, c, w_ada, b_ada, g_mix, g_ffn, w_in, w_fourier, conv_w, conv_b, lru_w_a, lru_b_a,
           lru_w_x, lru_b_x, lru_lambda, na_rpb, w_out, w_ffn_gate, w_ffn_up, w_ffn_down, g_final):
    b, s, d = x.shape
    depth = w_in.shape[0]
    n_groups = w_fourier.shape[1]
    d_f = n_groups * HEAD_DIM
    d_l = conv_w.shape[-1]
    d_n = na_rpb.shape[1] * HEAD_DIM
    tm = TOKEN_TILE

    mod = _modulation(c, w_ada, b_ada).reshape(depth, b, N_MOD, d)
    tables = _fourier_tables(n_groups)
    splits = _in_proj_splits(d_f, d_l, d_n)
    w_in_b = w_in.astype(BF16)
    w_out_b = w_out.astype(BF16)
    w_gate_b = w_ffn_gate.astype(BF16)
    w_up_b = w_ffn_up.astype(BF16)
    w_down_b = w_ffn_down.astype(BF16)
    g_mix3 = g_mix.reshape(depth, 1, d)
    g_ffn3 = g_ffn.reshape(depth, 1, d)
    w_fourier_b = _block_diag(w_fourier).astype(BF16)
    w_gates = jnp.concatenate([_block_diag(0.5 * lru_w_a), _block_diag(0.5 * lru_w_x)],
                              axis=-1).astype(BF16)
    conv_b3 = conv_b.reshape(depth, 1, d_l)
    na_bias = _na_bias(na_rpb)

    p_f, p_x, p_g, q, k, v = _in_proj(x, mod, g_mix3, w_in_b, 0, splits)
    for l in range(depth):
        y_f = _fourier_mix(p_f, w_fourier_b, l, tables)
        s_lru, p_lru, h_next = _rg_lru(p_x, conv_w, conv_b3, w_gates, lru_b_a, lru_b_x, lru_lambda, l)
        h_next = jnp.transpose(h_next, (1, 0, 2)).reshape(b, s // tm, tm // LRU_TILE, d_l)
        y_n = _neighbourhood_attention(q, k, v, na_bias, l)
        common = (y_f, s_lru, p_lru, h_next, p_g, y_n, x, mod, w_out_b, g_ffn3,
                  w_gate_b, w_up_b, w_down_b, l)
        if l + 1 < depth:
            x, p_f, p_x, p_g, q, k, v = _layer_tail(*common, nxt=(g_mix3, w_in_b), splits=splits)
        else:
            (x,) = _layer_tail(*common, g_final=g_final.reshape(1, d))
    return x
```

```python
import functools

import numpy as np
import jax
import jax.numpy as jnp
from jax import lax
from jax.experimental import pallas as pl
from jax.experimental.pallas import tpu as pltpu

F32 = jnp.float32
BF16 = jnp.bfloat16

HEAD_DIM = 64
GRID_W = 64
CONV_W = 4
LRU_C = 8.0
NA_KH = 8
NA_KW = 16
N_MOD = 6
EPS = 1e-6
NEG_BIAS = -1e30
LOG2E = 1.4426950408889634
NA_ROWS_PER_STEP = 2

LANES = 128
SUBLANES = 8
VMEM_LIMIT = 56 * 1024 * 1024
TAIL_VMEM_LIMIT = 60 * 1024 * 1024

TOKEN_TILE = 512
FF_CHUNK = 256
LRU_TILE = 128
LRU_GROUP = 8


def _params(*sem):
    return pltpu.CompilerParams(dimension_semantics=sem, vmem_limit_bytes=VMEM_LIMIT)


def _const_spec(shape):
    nd = len(shape)
    return pl.BlockSpec(shape, lambda *_: (0,) * nd, pipeline_mode=pl.Buffered(1))


def _layer_spec(arr, l):
    nd = arr.ndim - 1
    return pl.BlockSpec((None,) + arr.shape[1:], lambda *_: (l,) + (0,) * nd,
                        pipeline_mode=pl.Buffered(1))


def _mod_spec(mod, l):
    return pl.BlockSpec((None, 1) + mod.shape[2:], lambda i, j: (l, i, 0, 0))


def _norm_mod(x, g, shift, scale):
    ms = jnp.mean(x * x, axis=-1, keepdims=True)
    return (x * lax.rsqrt(ms + EPS)) * (g * (1.0 + scale)) + shift


def _gelu_tanh(x):
    return 0.5 * x * (1.0 + jnp.tanh(0.7978845608028654 * (x + 0.044715 * (x * x * x))))


def _mod_kernel(c_ref, w_ref, b_ref, o_ref):
    c = c_ref[...]
    ca = c * jax.nn.sigmoid(c)
    o_ref[0] = jnp.dot(ca, w_ref[0], precision=lax.Precision.HIGHEST,
                       preferred_element_type=F32) + b_ref[0]


def _modulation(c, w_ada, b_ada):
    depth, d, n = w_ada.shape
    b = c.shape[0]
    tn = 1024
    return pl.pallas_call(
        _mod_kernel,
        grid=(depth, n // tn),
        in_specs=[pl.BlockSpec((b, d), lambda l, j: (0, 0)),
                  pl.BlockSpec((1, d, tn), lambda l, j: (l, 0, j)),
                  pl.BlockSpec((1, 1, tn), lambda l, j: (l, 0, j))],
        out_specs=pl.BlockSpec((1, b, tn), lambda l, j: (l, 0, j)),
        out_shape=jax.ShapeDtypeStruct((depth, b, n), F32),
        compiler_params=_params("arbitrary", "arbitrary"),
        name="adaln_mod",
    )(c, w_ada, b_ada.reshape(depth, 1, n))


ROW_PITCH = GRID_W + SUBLANES


def _in_proj_splits(d_f, d_l, d_n):
    o1 = d_f
    o2 = o1 + d_l
    o3 = o2 + d_l
    o4 = o3 + d_n
    o5 = o4 + d_n
    o6 = o5 + d_n
    return ((0, o1, 1.0), (o1, o2, 1.0), (o2, o3, 1.0),
            (o3, o4, HEAD_DIM ** -0.5 * LOG2E), (o4, o5, 1.0), (o5, o6, 1.0))


def _in_proj_out(b, s, tm, splits):
    (lo, hi, _), rest = splits[0], splits[1:]
    rows = s // GRID_W
    shapes = [jax.ShapeDtypeStruct((b, (hi - lo) // LANES, rows * ROW_PITCH, LANES), F32)]
    specs = [pl.BlockSpec((1, (hi - lo) // LANES, tm // GRID_W * ROW_PITCH, LANES),
                          lambda i, j: (i, 0, j, 0))]
    for lo, hi, _ in rest:
        shapes.append(jax.ShapeDtypeStruct((b, s, hi - lo), BF16))
        specs.append(pl.BlockSpec((1, tm, hi - lo), lambda i, j: (i, j, 0)))
    return shapes, specs


IN_PROJ_GROUPS = ((1, 2), (3, 4), (5,))


def _emit_in_proj(h, w_ref, out_refs, splits, row0=0):
    pf_ref = out_refs[0]
    lo, hi, _ = splits[0]
    n_rows = h.shape[0]
    p = jnp.dot(h, w_ref[:, lo:hi], preferred_element_type=F32)
    pad = jnp.zeros((ROW_PITCH - GRID_W, LANES), F32)
    for lt in range((hi - lo) // LANES):
        for a in range(n_rows // GRID_W):
            dst = (row0 // GRID_W + a) * ROW_PITCH
            pf_ref[0, lt, dst:dst + GRID_W, :] = p[a * GRID_W:(a + 1) * GRID_W, lt * LANES:(lt + 1) * LANES]
            pf_ref[0, lt, dst + GRID_W:dst + ROW_PITCH, :] = pad
    for group in IN_PROJ_GROUPS:
        g_lo, g_hi = splits[group[0]][0], splits[group[-1]][1]
        p = jnp.dot(h, w_ref[:, g_lo:g_hi], preferred_element_type=F32)
        for idx in group:
            lo, hi, scale = splits[idx]
            part = p[:, lo - g_lo:hi - g_lo]
            if scale != 1.0:
                part = part * scale
            out_refs[idx][0, row0:row0 + n_rows, :] = part.astype(out_refs[idx].dtype)


def _in_proj_kernel(x_ref, mod_ref, g_ref, w_ref, *out_refs, splits):
    h = _norm_mod(x_ref[0], g_ref[...], mod_ref[0, 0:1, :], mod_ref[0, 1:2, :]).astype(BF16)
    _emit_in_proj(h, w_ref, out_refs, splits)


def _in_proj(x, mod, g_mix, w_in, l, splits):
    b, s, d = x.shape
    tm = TOKEN_TILE
    shapes, specs = _in_proj_out(b, s, tm, splits)
    return pl.pallas_call(
        functools.partial(_in_proj_kernel, splits=splits),
        grid=(b, s // tm),
        in_specs=[pl.BlockSpec((1, tm, d), lambda i, j: (i, j, 0)),
                  _mod_spec(mod, l), _layer_spec(g_mix, l), _layer_spec(w_in, l)],
        out_specs=specs,
        out_shape=shapes,
        compiler_params=_params("parallel", "arbitrary"),
        name="in_proj",
    )(x, mod, g_mix, w_in)


FOURIER_SLAB = 2 * GRID_W + SUBLANES
FOURIER_COLS = 8


def _fourier_tables(n_groups):
    n = GRID_W
    idx = np.arange(n)
    ang1 = 2.0 * np.pi * np.outer(idx, idx) / n
    f1 = np.concatenate([np.cos(ang1), np.sin(ang1)], axis=0)
    c = idx[:, None, None]
    dd = idx[None, :, None]
    bb = idx[None, None, :]
    ang2 = 2.0 * np.pi * bb * (n * dd + c) / (n * n)
    gc, gs = np.cos(ang2), np.sin(ang2)
    g2 = np.concatenate([np.concatenate([gc, -gs], axis=2),
                         np.concatenate([gs, gc], axis=2)], axis=1)
    angd = 2.0 * np.pi * np.outer(np.arange(HEAD_DIM), np.arange(HEAD_DIM)) / HEAD_DIM
    eye = np.eye(n_groups)
    cs = np.concatenate([np.kron(eye, np.cos(angd)), -np.kron(eye, np.sin(angd))], axis=0)
    return jnp.asarray(f1, BF16), jnp.asarray(g2, BF16), jnp.asarray(cs, BF16)


def _fourier_kernel(x_ref, f1_ref, g2_ref, cs_ref, w_ref, o_ref, z_ref, y_ref, *, width, scale):
    n = GRID_W
    n_lt = width // LANES
    f1 = f1_ref[...]
    for b in range(n):
        xb = jnp.concatenate([x_ref[0, lt, pl.ds(b, n, stride=ROW_PITCH), :] for lt in range(n_lt)],
                             axis=1).astype(BF16)
        slab = jnp.dot(f1, xb, preferred_element_type=F32)
        for lt in range(n_lt):
            z_ref[lt, b * FOURIER_SLAB:b * FOURIER_SLAB + 2 * n, :] = slab[:, lt * LANES:(lt + 1) * LANES]
    for c in range(n):
        z = jnp.concatenate(
            [jnp.concatenate([z_ref[lt, pl.ds(off + c, n, stride=FOURIER_SLAB), :]
                              for lt in range(n_lt)], axis=1)
             for off in (0, n)], axis=0).astype(BF16)
        y = jnp.dot(g2_ref[c], z, preferred_element_type=F32)
        y_ref[c * n:(c + 1) * n, 0:width] = y[0:n].astype(BF16)
        y_ref[c * n:(c + 1) * n, width:2 * width] = y[n:2 * n].astype(BF16)
    cs = cs_ref[...]
    w = w_ref[...]
    for blk in range(n // FOURIER_COLS):
        rows = slice(blk * FOURIER_COLS * n, (blk + 1) * FOURIER_COLS * n)
        f = jnp.dot(y_ref[rows, :], cs, preferred_element_type=F32) * scale
        out = jnp.dot(f.astype(BF16), w, preferred_element_type=F32)
        for cc in range(FOURIER_COLS):
            c = blk * FOURIER_COLS + cc
            for lt in range(n_lt):
                o_ref[0, lt, pl.ds(c, n, stride=ROW_PITCH), :] = \
                    out[cc * n:(cc + 1) * n, lt * LANES:(lt + 1) * LANES]
    pad = jnp.zeros((ROW_PITCH - n, LANES), F32)
    for lt in range(n_lt):
        for dd in range(n):
            o_ref[0, lt, dd * ROW_PITCH + n:(dd + 1) * ROW_PITCH, :] = pad


def _fourier_mix(p_f, w_blk, l, tables):
    b, n_lt, prow, _ = p_f.shape
    n = GRID_W
    assert prow == n * ROW_PITCH, "sequence DFT is factored as 64 x 64"
    width = n_lt * LANES
    f1, g2, cs = tables
    scale = float(1.0 / np.sqrt(n * n * HEAD_DIM))
    blk = pl.BlockSpec((1, n_lt, prow, LANES), lambda i: (i, 0, 0, 0))
    return pl.pallas_call(
        functools.partial(_fourier_kernel, width=width, scale=scale),
        grid=(b,),
        in_specs=[blk,
                  _const_spec(f1.shape), _const_spec(g2.shape),
                  _const_spec(cs.shape), _layer_spec(w_blk, l)],
        out_specs=blk,
        out_shape=jax.ShapeDtypeStruct(p_f.shape, F32),
        scratch_shapes=[pltpu.VMEM((n_lt, n * FOURIER_SLAB, LANES), F32),
                        pltpu.VMEM((n * n, 2 * width), BF16)],
        compiler_params=_params("parallel"),
        name="fourier_mix",
    )(p_f, f1, g2, cs, w_blk)


LRU_HALO = 2 * SUBLANES
LRU_XSTRIDE = LRU_TILE + 2 * LRU_HALO + SUBLANES
LRU_CONV_LEFT = 2
LRU_CHUNK = 64
LRU_BIAS_ROWS = 2


def _softplus_neg(lam):
    e = jnp.exp(-jnp.abs(lam))
    u = 1.0 + e
    l1p = jnp.where(u == 1.0, e, jnp.log(u) * (e / (u - 1.0)))
    return jnp.maximum(-lam, 0.0) + l1p


def _lru_kernel(x_ref, xp_ref, xn_ref, cw_ref, cb_ref, w_ref, lam_ref,
                s_ref, p_ref, carry_ref,
                xs_ref, xt_ref, af_ref, bf_ref, ab_ref, bb_ref, hs_ref, ps_ref,
                cf_ref, lsum_ref, psum_ref, *, nt):
    ts, grp = LRU_TILE, LRU_GROUP
    n_lt = xs_ref.shape[0]
    c = n_lt * LANES
    lanes = lambda lt: slice(lt * LANES, (lt + 1) * LANES)
    i = pl.program_id(1)
    first = i == 0
    last = i == nt - 1

    @pl.when(first)
    def _():
        cf_ref[...] = jnp.zeros_like(cf_ref)

    for j in range(grp):
        base = j * LRU_XSTRIDE
        for lt in range(n_lt):
            xs_ref[lt, base:base + LRU_HALO, :] = \
                jnp.where(first, 0.0, xp_ref[j, :, lanes(lt)].astype(F32))
            xs_ref[lt, base + LRU_HALO:base + LRU_HALO + ts, :] = x_ref[j, :, lanes(lt)].astype(F32)
            xs_ref[lt, base + LRU_HALO + ts:base + 2 * LRU_HALO + ts, :] = \
                jnp.where(last, 0.0, xn_ref[j, :, lanes(lt)].astype(F32))

    def to_time_major(tt, carry):
        r0 = pl.multiple_of(tt * grp, grp)
        src = pl.ds(LRU_HALO - LRU_CONV_LEFT + tt, grp, stride=LRU_XSTRIDE)
        for lt in range(n_lt):
            xt_ref[pl.ds(r0, grp), lanes(lt)] = xs_ref[lt, src, :]
        return carry

    lax.fori_loop(0, ts + CONV_W - 1, to_time_major, 0, unroll=8)

    cw = cw_ref[...]
    cb = cb_ref[...]
    coef = [(0.5 * LRU_C) * _softplus_neg(lam_ref[d:d + 1, :]) for d in range(2)]
    rows_c = LRU_CHUNK * grp
    bias_cols = jnp.where(lax.broadcasted_iota(jnp.int32, (rows_c, LANES), 1) < LRU_BIAS_ROWS,
                          1.0, 0.0).astype(BF16)

    def gate_chunk(ch, carry):
        r0 = pl.multiple_of(ch * rows_c, rows_c)
        u = cb
        for k in range(CONV_W):
            u = u + cw[k:k + 1, :] * xt_ref[pl.ds(r0 + k * grp, rows_c), :]
        ub = jnp.concatenate([u.astype(BF16), bias_cols], axis=1)
        uh = 0.5 * u
        for d, (a_ref, b_ref) in enumerate(((af_ref, bf_ref), (ab_ref, bb_ref))):
            z = jnp.dot(ub, w_ref[d], preferred_element_type=F32)
            tha = jnp.tanh(z[:, 0:c])
            thx = jnp.tanh(z[:, c:2 * c])
            nla = coef[d] * (tha + 1.0)
            a = jnp.exp2(nla * (-LOG2E))
            m = jnp.tanh(nla) * (1.0 + a * a)
            sq = jnp.where(m > 0.0, m * lax.rsqrt(m), 0.0)
            a_ref[pl.ds(r0, rows_c), :] = a
            b_ref[pl.ds(r0, rows_c), :] = sq * ((thx + 1.0) * uh)
        return carry

    lax.fori_loop(0, ts // LRU_CHUNK, gate_chunk, 0)

    def bwd_step(s, carry):
        h, p = carry
        rows = pl.ds(pl.multiple_of((ts - 1 - s) * grp, grp), grp)
        a = [ab_ref[rows, lanes(lt)] for lt in range(n_lt)]
        h = tuple(a[lt] * h[lt] + bb_ref[rows, lanes(lt)] for lt in range(n_lt))
        p = tuple(a[lt] * p[lt] for lt in range(n_lt))
        for lt in range(n_lt):
            hs_ref[lt, rows, :] = h[lt]
            ps_ref[lt, rows, :] = p[lt]
        return h, p

    zero = jnp.zeros((grp, LANES), F32)
    h, p = lax.fori_loop(0, ts, bwd_step, ((zero,) * n_lt, (zero + 1.0,) * n_lt), unroll=8)
    for lt in range(n_lt):
        lsum_ref[i, :, lanes(lt)] = h[lt]
        psum_ref[i, :, lanes(lt)] = p[lt]

    def fwd_step(t, h):
        rows = pl.ds(pl.multiple_of(t * grp, grp), grp)
        h = tuple(af_ref[rows, lanes(lt)] * h[lt] + bf_ref[rows, lanes(lt)] for lt in range(n_lt))
        for lt in range(n_lt):
            hs_ref[lt, rows, :] = hs_ref[lt, rows, :] + h[lt]
        return h

    h = lax.fori_loop(0, ts, fwd_step, tuple(cf_ref[:, lanes(lt)] for lt in range(n_lt)), unroll=8)
    for lt in range(n_lt):
        cf_ref[:, lanes(lt)] = h[lt]

    pack = 2 * SUBLANES
    for src_ref, o_ref in ((hs_ref, s_ref), (ps_ref, p_ref)):
        for j in range(grp):
            for tb in range(ts // pack):
                for lt in range(n_lt):
                    o_ref[j, tb * pack:(tb + 1) * pack, lanes(lt)] = jnp.concatenate(
                        [src_ref[lt, pl.ds((tb * pack + k * SUBLANES) * grp + j, SUBLANES, stride=grp), :]
                         for k in range(2)], axis=0).astype(o_ref.dtype)

    @pl.when(last)
    def _():
        hn = jnp.zeros((grp, c), F32)
        for t in reversed(range(nt)):
            carry_ref[t] = hn
            hn = lsum_ref[t] + psum_ref[t] * hn


def _gate_weights(w_a, b_a, w_x, b_x):
    w = jnp.concatenate([_block_diag(0.5 * w_a), _block_diag(0.5 * w_x)], axis=-1)
    b = 0.5 * jnp.concatenate([b_a, b_x], axis=-1)
    b_hi = b.astype(BF16).astype(F32)
    extra = jnp.stack([b_hi, b - b_hi], axis=-2)
    extra = jnp.pad(extra, [(0, 0), (0, 0), (0, LANES - LRU_BIAS_ROWS), (0, 0)])
    return jnp.concatenate([w, extra], axis=-2).astype(BF16)


def _rg_lru(p_x, conv_w, conv_b, w_gates, lam, l):
    b, s, c = p_x.shape
    ts, grp, halo = LRU_TILE, LRU_GROUP, LRU_HALO
    assert b % grp == 0 and s % ts == 0
    assert grp == SUBLANES and ts % LRU_CHUNK == 0 and c % LANES == 0
    nt = s // ts
    hb_per_tile = ts // halo
    n_halo = s // halo
    n_lt = c // LANES
    cur = pl.BlockSpec((grp, ts, c), lambda g, i: (g, i, 0))
    prev = pl.BlockSpec((grp, halo, c), lambda g, i: (g, jnp.maximum(i * hb_per_tile - 1, 0), 0))
    nxt = pl.BlockSpec((grp, halo, c),
                       lambda g, i: (g, jnp.minimum((i + 1) * hb_per_tile, n_halo - 1), 0))
    tile_f32 = pltpu.VMEM((ts * grp, c), F32)
    planes_f32 = pltpu.VMEM((n_lt, ts * grp, LANES), F32)
    return pl.pallas_call(
        functools.partial(_lru_kernel, nt=nt),
        grid=(b // grp, nt),
        in_specs=[cur, prev, nxt,
                  _layer_spec(conv_w, l), _layer_spec(conv_b, l), _layer_spec(w_gates, l),
                  _layer_spec(lam, l)],
        out_specs=[cur, cur, pl.BlockSpec((nt, grp, c), lambda g, i: (0, g, 0))],
        out_shape=[jax.ShapeDtypeStruct((b, s, c), BF16), jax.ShapeDtypeStruct((b, s, c), BF16),
                   jax.ShapeDtypeStruct((nt, b, c), F32)],
        scratch_shapes=[pltpu.VMEM((n_lt, grp * LRU_XSTRIDE, LANES), F32),
                        pltpu.VMEM(((ts + CONV_W - 1) * grp, c), F32),
                        tile_f32, tile_f32, tile_f32, tile_f32,
                        planes_f32, planes_f32,
                        pltpu.VMEM((grp, c), F32),
                        pltpu.VMEM((nt, grp, c), F32),
                        pltpu.VMEM((nt, grp, c), F32)],
        compiler_params=_params("parallel", "arbitrary"),
        name="rg_lru",
    )(p_x, p_x, p_x, conv_w, conv_b, w_gates, lam)


def _block_diag(w):
    h, dh = w.shape[-3], w.shape[-1]
    lead = [(0, 0)] * (w.ndim - 2)
    rows = [jnp.pad(w[..., k, :, :], lead + [(k * dh, (h - 1 - k) * dh)]) for k in range(h)]
    return jnp.concatenate(rows, axis=-2)


def _na_bias(rpb):
    n_l, h = rpb.shape[:2]
    w, kh, kw = GRID_W, NA_KH, NA_KW
    qc = np.arange(w)[:, None]
    kc = np.arange(w)[None, :]
    cs = np.clip(qc - kw // 2, 0, w - kw)
    inwin = (kc >= cs) & (kc < cs + kw)
    sel_c = (inwin[:, :, None]
             & (np.arange(2 * kw - 1)[None, None, :] == (kc - qc + kw - 1)[:, :, None])).astype(np.float32)
    var = np.arange(kh)[:, None]
    ii = np.arange(kh)[None, :]
    sel_r = (np.arange(2 * kh - 1)[None, None, :] == (ii - var + kh - 1)[:, :, None]).astype(np.float32)
    t = jnp.einsum('lhrc,vir,qkc->lvhqik', rpb * LOG2E, sel_r, sel_c,
                   precision=lax.Precision.HIGHEST)
    t = t + np.where(inwin, 0.0, NEG_BIAS).astype(np.float32)[:, None, :]
    return t.reshape(n_l, kh, h // 2, 2 * w, kh * w)


def _na_kernel(q_ref, k_ref, v_ref, bias_ref, o_ref, *, rows, n_pairs):
    w = GRID_W
    band = NA_KH * w
    lane = lax.broadcasted_iota(jnp.int32, (w, LANES), 1)
    low = lane < HEAD_DIM

    def body(it, carry):
        chains = []
        for rr in range(NA_ROWS_PER_STEP):
            r = it * NA_ROWS_PER_STEP + rr
            rs = jnp.clip(r - NA_KH // 2, 0, rows - NA_KH)
            q0 = pl.multiple_of(r * w, w)
            k0 = pl.multiple_of(rs * w, w)
            for pr in range(n_pairs):
                chains.append((r - rs, q0, k0, pr, slice(pr * LANES, (pr + 1) * LANES)))
        scores = []
        for var, q0, k0, pr, sl in chains:
            q2 = q_ref[0, pl.ds(q0, w), sl]
            zero = jnp.zeros_like(q2)
            qq = jnp.concatenate([jnp.where(low, q2, zero), jnp.where(low, zero, q2)], axis=0)
            kk = k_ref[0, pl.ds(k0, band), sl]
            s = lax.dot_general(qq, kk, (((1,), (1,)), ((), ())), preferred_element_type=F32)
            scores.append(s + bias_ref[var, pr])
        probs = []
        for s in scores:
            e = jnp.exp2(s - jnp.max(s, axis=-1, keepdims=True))
            probs.append((e.astype(BF16), jnp.sum(e, axis=-1, keepdims=True)))
        for (var, q0, k0, pr, sl), (e, l) in zip(chains, probs):
            vv = v_ref[0, pl.ds(k0, band), sl]
            pv = jnp.dot(e, vv, preferred_element_type=F32) / l
            o = jnp.where(low, pv[0:w], pv[w:2 * w])
            o_ref[0, pl.ds(q0, w), sl] = o.astype(o_ref.dtype)
        return carry

    lax.fori_loop(0, rows // NA_ROWS_PER_STEP, body, 0)


def _neighbourhood_attention(q, k, v, bias, l):
    b, s, c = q.shape
    rows = s // GRID_W
    assert rows >= NA_KH and rows % NA_ROWS_PER_STEP == 0 and c % LANES == 0
    n_pairs = c // LANES
    seq = pl.BlockSpec((1, s, c), lambda i: (i, 0, 0))
    return pl.pallas_call(
        functools.partial(_na_kernel, rows=rows, n_pairs=n_pairs),
        grid=(b,),
        in_specs=[seq, seq, seq, _layer_spec(bias, l)],
        out_specs=seq,
        out_shape=jax.ShapeDtypeStruct((b, s, c), BF16),
        compiler_params=_params("parallel"),
        name="neighbourhood_attn",
    )(q, k, v, bias)


def _tail_kernel(yf_ref, s_ref, p_ref, hn_ref, pg_ref, yn_ref, x_ref, mod_ref, wo_ref,
                 g_ref, wg_ref, wu_ref, wd_ref, *rest, d_ff, splits):
    if splits is None:
        gf_ref, o_ref, act_ref = rest
    else:
        modn_ref, gn_ref, win_ref, o_ref = rest[:4]
        p_refs, act_ref = rest[4:-1], rest[-1]
    tm = x_ref.shape[1]
    n_lt = yf_ref.shape[1]
    ts = LRU_TILE

    h_bwd_tail = jnp.concatenate(
        [p_ref[0, k * ts:(k + 1) * ts, :].astype(F32) * hn_ref[0, 0, k:k + 1, :]
         for k in range(tm // ts)], axis=0)
    yl = (_gelu_tanh(pg_ref[0].astype(F32)) * (s_ref[0].astype(F32) + h_bwd_tail)).astype(BF16)
    yf = [jnp.concatenate([yf_ref[0, lt, a * ROW_PITCH:a * ROW_PITCH + GRID_W, :]
                           for a in range(tm // GRID_W)], axis=0).astype(BF16) for lt in range(n_lt)]
    mixed = jnp.concatenate(yf + [yl, yn_ref[0]], axis=1)
    y = jnp.dot(mixed, wo_ref[...], preferred_element_type=F32)
    o_ref[0] = x_ref[0] + mod_ref[0, 2:3, :] * y

    h = _norm_mod(o_ref[0], g_ref[...], mod_ref[0, 3:4, :], mod_ref[0, 4:5, :]).astype(BF16)
    for j in range(d_ff // FF_CHUNK):
        cs = slice(j * FF_CHUNK, (j + 1) * FF_CHUNK)
        gt = jnp.dot(h, wg_ref[:, cs], preferred_element_type=F32)
        up = jnp.dot(h, wu_ref[:, cs], preferred_element_type=F32)
        act_ref[:, cs] = (gt * jax.nn.sigmoid(gt) * up).astype(BF16)
    f = jnp.dot(act_ref[...], wd_ref[...], preferred_element_type=F32)
    x = o_ref[0] + mod_ref[0, 5:6, :] * f
    if splits is None:
        ms = jnp.mean(x * x, axis=-1, keepdims=True)
        o_ref[0] = x * lax.rsqrt(ms + EPS) * gf_ref[...]
    else:
        o_ref[0] = x
        hn = _norm_mod(x, gn_ref[...], modn_ref[0, 0:1, :], modn_ref[0, 1:2, :]).astype(BF16)
        _emit_in_proj(hn, win_ref, p_refs, splits)


def _layer_tail(y_f, s_lru, p_lru, h_next, p_g, y_n, x, mod, w_out, g_ffn, w_gate, w_up, w_down, l,
                nxt=None, g_final=None, splits=None):
    b, s, d = x.shape
    tm = TOKEN_TILE
    d_ff = w_gate.shape[-1]
    assert d_ff % FF_CHUNK == 0 and tm % GRID_W == 0 and tm % LRU_TILE == 0
    tok = lambda w: pl.BlockSpec((1, tm, w), lambda i, j: (i, j, 0))
    in_specs = [pl.BlockSpec((1, y_f.shape[1], tm // GRID_W * ROW_PITCH, LANES), lambda i, j: (i, 0, j, 0)),
                tok(s_lru.shape[-1]), tok(p_lru.shape[-1]),
                pl.BlockSpec((1, 1) + h_next.shape[2:], lambda i, j: (i, j, 0, 0)),
                tok(p_g.shape[-1]), tok(y_n.shape[-1]), tok(d),
                _mod_spec(mod, l), _layer_spec(w_out, l), _layer_spec(g_ffn, l),
                _layer_spec(w_gate, l), _layer_spec(w_up, l), _layer_spec(w_down, l)]
    args = [y_f, s_lru, p_lru, h_next, p_g, y_n, x, mod, w_out, g_ffn, w_gate, w_up, w_down]
    out_shape = [jax.ShapeDtypeStruct((b, s, d), F32)]
    out_specs = [tok(d)]
    if nxt is None:
        in_specs += [_const_spec(g_final.shape)]
        args += [g_final]
        splits = None
    else:
        g_mix, w_in = nxt
        in_specs += [_mod_spec(mod, l + 1), _layer_spec(g_mix, l + 1), _layer_spec(w_in, l + 1)]
        args += [mod, g_mix, w_in]
        shapes, specs = _in_proj_out(b, s, tm, splits)
        out_shape += shapes
        out_specs += specs
    return pl.pallas_call(
        functools.partial(_tail_kernel, d_ff=d_ff, splits=splits),
        grid=(b, s // tm),
        in_specs=in_specs,
        out_specs=out_specs,
        out_shape=out_shape,
        scratch_shapes=[pltpu.VMEM((tm, d_ff), BF16)],
        compiler_params=pltpu.CompilerParams(dimension_semantics=("parallel", "arbitrary"),
                                             vmem_limit_bytes=TAIL_VMEM_LIMIT),
        name="layer_tail",
    )(*args)


def kernel(x, c, w_ada, b_ada, g_mix, g_ffn, w_in, w_fourier, conv_w, conv_b, lru_w_a, lru_b_a,
           lru_w_x, lru_b_x, lru_lambda, na_rpb, w_out, w_ffn_gate, w_ffn_up, w_ffn_down, g_final):
    b, s, d = x.shape
    depth = w_in.shape[0]
    n_groups = w_fourier.shape[1]
    d_f = n_groups * HEAD_DIM
    d_l = conv_w.shape[-1]
    d_n = na_rpb.shape[1] * HEAD_DIM
    tm = TOKEN_TILE

    mod = _modulation(c, w_ada, b_ada).reshape(depth, b, N_MOD, d)
    tables = _fourier_tables(n_groups)
    splits = _in_proj_splits(d_f, d_l, d_n)
    w_in_b = w_in.astype(BF16)
    w_out_b = w_out.astype(BF16)
    w_gate_b = w_ffn_gate.astype(BF16)
    w_up_b = w_ffn_up.astype(BF16)
    w_down_b = w_ffn_down.astype(BF16)
    g_mix3 = g_mix.reshape(depth, 1, d)
    g_ffn3 = g_ffn.reshape(depth, 1, d)
    w_fourier_b = _block_diag(w_fourier).astype(BF16)
    w_gates = _gate_weights(lru_w_a, lru_b_a, lru_w_x, lru_b_x)
    conv_b3 = conv_b.reshape(depth, 1, d_l)
    na_bias = _na_bias(na_rpb)

    p_f, p_x, p_g, q, k, v = _in_proj(x, mod, g_mix3, w_in_b, 0, splits)
    for l in range(depth):
        y_f = _fourier_mix(p_f, w_fourier_b, l, tables)
        s_lru, p_lru, h_next = _rg_lru(p_x, conv_w, conv_b3, w_gates, lru_lambda, l)
        h_next = jnp.transpose(h_next, (1, 0, 2)).reshape(b, s // tm, tm // LRU_TILE, d_l)
        y_n = _neighbourhood_attention(q, k, v, na_bias, l)
        common = (y_f, s_lru, p_lru, h_next, p_g, y_n, x, mod, w_out_b, g_ffn3,
                  w_gate_b, w_up_b, w_down_b, l)
        if l + 1 < depth:
            x, p_f, p_x, p_g, q, k, v = _layer_tail(*common, nxt=(g_mix3, w_in_b), splits=splits)
        else:
            (x,) = _layer_tail(*common, g_final=g_final.reshape(1, d))
    return x
```

```python
import functools

import numpy as np
import jax
import jax.numpy as jnp
from jax import lax
from jax.experimental import pallas as pl
from jax.experimental.pallas import tpu as pltpu

F32 = jnp.float32
BF16 = jnp.bfloat16

HEAD_DIM = 64
GRID_W = 64
CONV_W = 4
LRU_C = 8.0
NA_KH = 8
NA_KW = 16
N_MOD = 6
EPS = 1e-6
NEG_BIAS = -1e30
LOG2E = 1.4426950408889634
NA_ROWS_PER_STEP = 2

LANES = 128
SUBLANES = 8
VMEM_LIMIT = 56 * 1024 * 1024
TAIL_VMEM_LIMIT = 60 * 1024 * 1024

TOKEN_TILE = 512
FF_CHUNK = 256
LRU_TILE = 128
LRU_GROUP = 8


def _params(*sem):
    return pltpu.CompilerParams(dimension_semantics=sem, vmem_limit_bytes=VMEM_LIMIT)


def _const_spec(shape):
    nd = len(shape)
    return pl.BlockSpec(shape, lambda *_: (0,) * nd, pipeline_mode=pl.Buffered(1))


def _layer_spec(arr, l):
    nd = arr.ndim - 1
    return pl.BlockSpec((None,) + arr.shape[1:], lambda *_: (l,) + (0,) * nd,
                        pipeline_mode=pl.Buffered(1))


def _mod_spec(mod, l):
    return pl.BlockSpec((None, 1) + mod.shape[2:], lambda i, j: (l, i, 0, 0))


def _norm_mod(x, g, shift, scale):
    ms = jnp.mean(x * x, axis=-1, keepdims=True)
    return (x * lax.rsqrt(ms + EPS)) * (g * (1.0 + scale)) + shift


def _gelu_tanh(x):
    return 0.5 * x * (1.0 + jnp.tanh(0.7978845608028654 * (x + 0.044715 * (x * x * x))))


def _mod_kernel(c_ref, w_ref, b_ref, o_ref):
    c = c_ref[...]
    ca = c * jax.nn.sigmoid(c)
    o_ref[0] = jnp.dot(ca, w_ref[0], precision=lax.Precision.HIGHEST,
                       preferred_element_type=F32) + b_ref[0]


def _modulation(c, w_ada, b_ada):
    depth, d, n = w_ada.shape
    b = c.shape[0]
    tn = 1024
    return pl.pallas_call(
        _mod_kernel,
        grid=(depth, n // tn),
        in_specs=[pl.BlockSpec((b, d), lambda l, j: (0, 0)),
                  pl.BlockSpec((1, d, tn), lambda l, j: (l, 0, j)),
                  pl.BlockSpec((1, 1, tn), lambda l, j: (l, 0, j))],
        out_specs=pl.BlockSpec((1, b, tn), lambda l, j: (l, 0, j)),
        out_shape=jax.ShapeDtypeStruct((depth, b, n), F32),
        compiler_params=_params("arbitrary", "arbitrary"),
        name="adaln_mod",
    )(c, w_ada, b_ada.reshape(depth, 1, n))


ROW_PITCH = GRID_W + SUBLANES


def _in_proj_splits(d_f, d_l, d_n):
    o1 = d_f
    o2 = o1 + d_l
    o3 = o2 + d_l
    o4 = o3 + d_n
    o5 = o4 + d_n
    o6 = o5 + d_n
    return ((0, o1, 1.0), (o1, o2, 1.0), (o2, o3, 1.0),
            (o3, o4, HEAD_DIM ** -0.5 * LOG2E), (o4, o5, 1.0), (o5, o6, 1.0))


def _in_proj_out(b, s, tm, splits, tile_of=lambda i, j: (i, j)):
    (lo, hi, _), rest = splits[0], splits[1:]
    rows = s // GRID_W

    def pf_map(*g):
        i, j = tile_of(*g)
        return i, 0, j, 0

    def tok_map(*g):
        i, j = tile_of(*g)
        return i, j, 0

    shapes = [jax.ShapeDtypeStruct((b, (hi - lo) // LANES, rows * ROW_PITCH, LANES), F32)]
    specs = [pl.BlockSpec((1, (hi - lo) // LANES, tm // GRID_W * ROW_PITCH, LANES), pf_map)]
    for lo, hi, _ in rest:
        shapes.append(jax.ShapeDtypeStruct((b, s, hi - lo), BF16))
        specs.append(pl.BlockSpec((1, tm, hi - lo), tok_map))
    return shapes, specs


IN_PROJ_GROUPS = ((1, 2), (3, 4), (5,))


def _emit_in_proj(h, w_ref, out_refs, splits, row0=0):
    pf_ref = out_refs[0]
    lo, hi, _ = splits[0]
    n_rows = h.shape[0]
    p = jnp.dot(h, w_ref[:, lo:hi], preferred_element_type=F32)
    pad = jnp.zeros((ROW_PITCH - GRID_W, LANES), F32)
    for lt in range((hi - lo) // LANES):
        for a in range(n_rows // GRID_W):
            dst = (row0 // GRID_W + a) * ROW_PITCH
            pf_ref[0, lt, dst:dst + GRID_W, :] = p[a * GRID_W:(a + 1) * GRID_W, lt * LANES:(lt + 1) * LANES]
            pf_ref[0, lt, dst + GRID_W:dst + ROW_PITCH, :] = pad
    for group in IN_PROJ_GROUPS:
        g_lo, g_hi = splits[group[0]][0], splits[group[-1]][1]
        p = jnp.dot(h, w_ref[:, g_lo:g_hi], preferred_element_type=F32)
        for idx in group:
            lo, hi, scale = splits[idx]
            part = p[:, lo - g_lo:hi - g_lo]
            if scale != 1.0:
                part = part * scale
            out_refs[idx][0, row0:row0 + n_rows, :] = part.astype(out_refs[idx].dtype)


def _in_proj_kernel(x_ref, mod_ref, g_ref, w_ref, *out_refs, splits):
    h = _norm_mod(x_ref[0], g_ref[...], mod_ref[0, 0:1, :], mod_ref[0, 1:2, :]).astype(BF16)
    _emit_in_proj(h, w_ref, out_refs, splits)


def _in_proj(x, mod, g_mix, w_in, l, splits):
    b, s, d = x.shape
    tm = TOKEN_TILE
    shapes, specs = _in_proj_out(b, s, tm, splits)
    return pl.pallas_call(
        functools.partial(_in_proj_kernel, splits=splits),
        grid=(b, s // tm),
        in_specs=[pl.BlockSpec((1, tm, d), lambda i, j: (i, j, 0)),
                  _mod_spec(mod, l), _layer_spec(g_mix, l), _layer_spec(w_in, l)],
        out_specs=specs,
        out_shape=shapes,
        compiler_params=_params("parallel", "arbitrary"),
        name="in_proj",
    )(x, mod, g_mix, w_in)


FOURIER_SLAB = 2 * GRID_W + SUBLANES
FOURIER_COLS = 8


def _fourier_tables(n_groups):
    n = GRID_W
    idx = np.arange(n)
    ang1 = 2.0 * np.pi * np.outer(idx, idx) / n
    f1 = np.concatenate([np.cos(ang1), np.sin(ang1)], axis=0)
    c = idx[:, None, None]
    dd = idx[None, :, None]
    bb = idx[None, None, :]
    ang2 = 2.0 * np.pi * bb * (n * dd + c) / (n * n)
    gc, gs = np.cos(ang2), np.sin(ang2)
    g2 = np.concatenate([np.concatenate([gc, -gs], axis=2),
                         np.concatenate([gs, gc], axis=2)], axis=1)
    angd = 2.0 * np.pi * np.outer(np.arange(HEAD_DIM), np.arange(HEAD_DIM)) / HEAD_DIM
    eye = np.eye(n_groups)
    cs = np.concatenate([np.kron(eye, np.cos(angd)), -np.kron(eye, np.sin(angd))], axis=0)
    return jnp.asarray(f1, BF16), jnp.asarray(g2, BF16), jnp.asarray(cs, BF16)


def _fourier_kernel(x_ref, f1_ref, g2_ref, cs_ref, w_ref, o_ref, z_ref, y_ref, *, width, scale):
    n = GRID_W
    n_lt = width // LANES
    f1 = f1_ref[...]
    for b in range(n):
        xb = jnp.concatenate([x_ref[0, lt, pl.ds(b, n, stride=ROW_PITCH), :] for lt in range(n_lt)],
                             axis=1).astype(BF16)
        slab = jnp.dot(f1, xb, preferred_element_type=F32)
        for lt in range(n_lt):
            z_ref[lt, b * FOURIER_SLAB:b * FOURIER_SLAB + 2 * n, :] = slab[:, lt * LANES:(lt + 1) * LANES]
    for c in range(n):
        z = jnp.concatenate(
            [jnp.concatenate([z_ref[lt, pl.ds(off + c, n, stride=FOURIER_SLAB), :]
                              for lt in range(n_lt)], axis=1)
             for off in (0, n)], axis=0).astype(BF16)
        y = jnp.dot(g2_ref[c], z, preferred_element_type=F32)
        y_ref[c * n:(c + 1) * n, 0:width] = y[0:n].astype(BF16)
        y_ref[c * n:(c + 1) * n, width:2 * width] = y[n:2 * n].astype(BF16)
    cs = cs_ref[...]
    w = w_ref[...]
    for blk in range(n // FOURIER_COLS):
        rows = slice(blk * FOURIER_COLS * n, (blk + 1) * FOURIER_COLS * n)
        f = jnp.dot(y_ref[rows, :], cs, preferred_element_type=F32) * scale
        out = jnp.dot(f.astype(BF16), w, preferred_element_type=F32)
        for cc in range(FOURIER_COLS):
            c = blk * FOURIER_COLS + cc
            for lt in range(n_lt):
                o_ref[0, lt, pl.ds(c, n, stride=ROW_PITCH), :] = \
                    out[cc * n:(cc + 1) * n, lt * LANES:(lt + 1) * LANES]
    pad = jnp.zeros((ROW_PITCH - n, LANES), F32)
    for lt in range(n_lt):
        for dd in range(n):
            o_ref[0, lt, dd * ROW_PITCH + n:(dd + 1) * ROW_PITCH, :] = pad


def _fourier_mix(p_f, w_blk, l, tables):
    b, n_lt, prow, _ = p_f.shape
    n = GRID_W
    assert prow == n * ROW_PITCH, "sequence DFT is factored as 64 x 64"
    width = n_lt * LANES
    f1, g2, cs = tables
    scale = float(1.0 / np.sqrt(n * n * HEAD_DIM))
    blk = pl.BlockSpec((1, n_lt, prow, LANES), lambda i: (i, 0, 0, 0))
    return pl.pallas_call(
        functools.partial(_fourier_kernel, width=width, scale=scale),
        grid=(b,),
        in_specs=[blk,
                  _const_spec(f1.shape), _const_spec(g2.shape),
                  _const_spec(cs.shape), _layer_spec(w_blk, l)],
        out_specs=blk,
        out_shape=jax.ShapeDtypeStruct(p_f.shape, F32),
        scratch_shapes=[pltpu.VMEM((n_lt, n * FOURIER_SLAB, LANES), F32),
                        pltpu.VMEM((n * n, 2 * width), BF16)],
        compiler_params=_params("parallel"),
        name="fourier_mix",
    )(p_f, f1, g2, cs, w_blk)


LRU_HALO = 2 * SUBLANES
LRU_XSTRIDE = LRU_TILE + 2 * LRU_HALO + SUBLANES
LRU_CONV_LEFT = 2
LRU_CHUNK = 64
LRU_BIAS_ROWS = 2


def _softplus_neg(lam):
    e = jnp.exp(-jnp.abs(lam))
    u = 1.0 + e
    l1p = jnp.where(u == 1.0, e, jnp.log(u) * (e / (u - 1.0)))
    return jnp.maximum(-lam, 0.0) + l1p


def _lru_kernel(x_ref, xp_ref, xn_ref, cw_ref, cb_ref, w_ref, lam_ref,
                s_ref, p_ref, carry_ref,
                xs_ref, xt_ref, af_ref, bf_ref, ab_ref, bb_ref, hs_ref, ps_ref,
                cf_ref, lsum_ref, psum_ref, *, nt):
    ts, grp = LRU_TILE, LRU_GROUP
    n_lt = xs_ref.shape[0]
    c = n_lt * LANES
    lanes = lambda lt: slice(lt * LANES, (lt + 1) * LANES)
    i = pl.program_id(1)
    first = i == 0
    last = i == nt - 1

    @pl.when(first)
    def _():
        cf_ref[...] = jnp.zeros_like(cf_ref)

    for j in range(grp):
        base = j * LRU_XSTRIDE
        for lt in range(n_lt):
            xs_ref[lt, base:base + LRU_HALO, :] = \
                jnp.where(first, 0.0, xp_ref[j, :, lanes(lt)].astype(F32))
            xs_ref[lt, base + LRU_HALO:base + LRU_HALO + ts, :] = x_ref[j, :, lanes(lt)].astype(F32)
            xs_ref[lt, base + LRU_HALO + ts:base + 2 * LRU_HALO + ts, :] = \
                jnp.where(last, 0.0, xn_ref[j, :, lanes(lt)].astype(F32))

    def to_time_major(tt, carry):
        r0 = pl.multiple_of(tt * grp, grp)
        src = pl.ds(LRU_HALO - LRU_CONV_LEFT + tt, grp, stride=LRU_XSTRIDE)
        for lt in range(n_lt):
            xt_ref[pl.ds(r0, grp), lanes(lt)] = xs_ref[lt, src, :]
        return carry

    lax.fori_loop(0, ts + CONV_W - 1, to_time_major, 0, unroll=8)

    cw = cw_ref[...]
    cb = cb_ref[...]
    coef = [(0.5 * LRU_C) * _softplus_neg(lam_ref[d:d + 1, :]) for d in range(2)]
    rows_c = LRU_CHUNK * grp
    bias_cols = jnp.where(lax.broadcasted_iota(jnp.int32, (rows_c, LANES), 1) < LRU_BIAS_ROWS,
                          1.0, 0.0).astype(BF16)

    def gate_chunk(ch, carry):
        r0 = pl.multiple_of(ch * rows_c, rows_c)
        u = cb
        for k in range(CONV_W):
            u = u + cw[k:k + 1, :] * xt_ref[pl.ds(r0 + k * grp, rows_c), :]
        ub = jnp.concatenate([u.astype(BF16), bias_cols], axis=1)
        uh = 0.5 * u
        for d, (a_ref, b_ref) in enumerate(((af_ref, bf_ref), (ab_ref, bb_ref))):
            z = jnp.dot(ub, w_ref[d], preferred_element_type=F32)
            tha = jnp.tanh(z[:, 0:c])
            thx = jnp.tanh(z[:, c:2 * c])
            nla = coef[d] * (tha + 1.0)
            a = jnp.exp2(nla * (-LOG2E))
            m = jnp.tanh(nla) * (1.0 + a * a)
            sq = jnp.where(m > 0.0, m * lax.rsqrt(m), 0.0)
            a_ref[pl.ds(r0, rows_c), :] = a
            b_ref[pl.ds(r0, rows_c), :] = sq * ((thx + 1.0) * uh)
        return carry

    lax.fori_loop(0, ts // LRU_CHUNK, gate_chunk, 0)

    def bwd_step(s, carry):
        h, p = carry
        rows = pl.ds(pl.multiple_of((ts - 1 - s) * grp, grp), grp)
        a = [ab_ref[rows, lanes(lt)] for lt in range(n_lt)]
        h = tuple(a[lt] * h[lt] + bb_ref[rows, lanes(lt)] for lt in range(n_lt))
        p = tuple(a[lt] * p[lt] for lt in range(n_lt))
        for lt in range(n_lt):
            hs_ref[lt, rows, :] = h[lt]
            ps_ref[lt, rows, :] = p[lt]
        return h, p

    zero = jnp.zeros((grp, LANES), F32)
    h, p = lax.fori_loop(0, ts, bwd_step, ((zero,) * n_lt, (zero + 1.0,) * n_lt), unroll=8)
    for lt in range(n_lt):
        lsum_ref[i, :, lanes(lt)] = h[lt]
        psum_ref[i, :, lanes(lt)] = p[lt]

    def fwd_step(t, h):
        rows = pl.ds(pl.multiple_of(t * grp, grp), grp)
        h = tuple(af_ref[rows, lanes(lt)] * h[lt] + bf_ref[rows, lanes(lt)] for lt in range(n_lt))
        for lt in range(n_lt):
            hs_ref[lt, rows, :] = hs_ref[lt, rows, :] + h[lt]
        return h

    h = lax.fori_loop(0, ts, fwd_step, tuple(cf_ref[:, lanes(lt)] for lt in range(n_lt)), unroll=8)
    for lt in range(n_lt):
        cf_ref[:, lanes(lt)] = h[lt]

    pack = 2 * SUBLANES
    for src_ref, o_ref in ((hs_ref, s_ref), (ps_ref, p_ref)):
        for j in range(grp):
            for tb in range(ts // pack):
                for lt in range(n_lt):
                    o_ref[j, tb * pack:(tb + 1) * pack, lanes(lt)] = jnp.concatenate(
                        [src_ref[lt, pl.ds((tb * pack + k * SUBLANES) * grp + j, SUBLANES, stride=grp), :]
                         for k in range(2)], axis=0).astype(o_ref.dtype)

    @pl.when(last)
    def _():
        hn = jnp.zeros((grp, c), F32)
        for t in reversed(range(nt)):
            carry_ref[t] = hn
            hn = lsum_ref[t] + psum_ref[t] * hn


def _gate_weights(w_a, b_a, w_x, b_x):
    w = jnp.concatenate([_block_diag(0.5 * w_a), _block_diag(0.5 * w_x)], axis=-1)
    b = 0.5 * jnp.concatenate([b_a, b_x], axis=-1)
    b_hi = b.astype(BF16).astype(F32)
    extra = jnp.stack([b_hi, b - b_hi], axis=-2)
    extra = jnp.pad(extra, [(0, 0), (0, 0), (0, LANES - LRU_BIAS_ROWS), (0, 0)])
    return jnp.concatenate([w, extra], axis=-2).astype(BF16)


def _rg_lru(p_x, conv_w, conv_b, w_gates, lam, l):
    b, s, c = p_x.shape
    ts, grp, halo = LRU_TILE, LRU_GROUP, LRU_HALO
    assert b % grp == 0 and s % ts == 0
    assert grp == SUBLANES and ts % LRU_CHUNK == 0 and c % LANES == 0
    nt = s // ts
    hb_per_tile = ts // halo
    n_halo = s // halo
    n_lt = c // LANES
    cur = pl.BlockSpec((grp, ts, c), lambda g, i: (g, i, 0))
    prev = pl.BlockSpec((grp, halo, c), lambda g, i: (g, jnp.maximum(i * hb_per_tile - 1, 0), 0))
    nxt = pl.BlockSpec((grp, halo, c),
                       lambda g, i: (g, jnp.minimum((i + 1) * hb_per_tile, n_halo - 1), 0))
    tile_f32 = pltpu.VMEM((ts * grp, c), F32)
    planes_f32 = pltpu.VMEM((n_lt, ts * grp, LANES), F32)
    return pl.pallas_call(
        functools.partial(_lru_kernel, nt=nt),
        grid=(b // grp, nt),
        in_specs=[cur, prev, nxt,
                  _layer_spec(conv_w, l), _layer_spec(conv_b, l), _layer_spec(w_gates, l),
                  _layer_spec(lam, l)],
        out_specs=[cur, cur, pl.BlockSpec((nt, grp, c), lambda g, i: (0, g, 0))],
        out_shape=[jax.ShapeDtypeStruct((b, s, c), BF16), jax.ShapeDtypeStruct((b, s, c), BF16),
                   jax.ShapeDtypeStruct((nt, b, c), F32)],
        scratch_shapes=[pltpu.VMEM((n_lt, grp * LRU_XSTRIDE, LANES), F32),
                        pltpu.VMEM(((ts + CONV_W - 1) * grp, c), F32),
                        tile_f32, tile_f32, tile_f32, tile_f32,
                        planes_f32, planes_f32,
                        pltpu.VMEM((grp, c), F32),
                        pltpu.VMEM((nt, grp, c), F32),
                        pltpu.VMEM((nt, grp, c), F32)],
        compiler_params=_params("parallel", "arbitrary"),
        name="rg_lru",
    )(p_x, p_x, p_x, conv_w, conv_b, w_gates, lam)


def _block_diag(w):
    h, dh = w.shape[-3], w.shape[-1]
    lead = [(0, 0)] * (w.ndim - 2)
    rows = [jnp.pad(w[..., k, :, :], lead + [(k * dh, (h - 1 - k) * dh)]) for k in range(h)]
    return jnp.concatenate(rows, axis=-2)


def _na_bias(rpb):
    n_l, h = rpb.shape[:2]
    w, kh, kw = GRID_W, NA_KH, NA_KW
    qc = np.arange(w)[:, None]
    kc = np.arange(w)[None, :]
    cs = np.clip(qc - kw // 2, 0, w - kw)
    inwin = (kc >= cs) & (kc < cs + kw)
    sel_c = (inwin[:, :, None]
             & (np.arange(2 * kw - 1)[None, None, :] == (kc - qc + kw - 1)[:, :, None])).astype(np.float32)
    var = np.arange(kh)[:, None]
    ii = np.arange(kh)[None, :]
    sel_r = (np.arange(2 * kh - 1)[None, None, :] == (ii - var + kh - 1)[:, :, None]).astype(np.float32)
    t = jnp.einsum('lhrc,vir,qkc->lvhqik', rpb * LOG2E, sel_r, sel_c,
                   precision=lax.Precision.HIGHEST)
    t = t + np.where(inwin, 0.0, NEG_BIAS).astype(np.float32)[:, None, :]
    return t.reshape(n_l, kh, h // 2, 2 * w, kh * w)


def _na_kernel(q_ref, k_ref, v_ref, bias_ref, o_ref, *, rows, n_pairs):
    w = GRID_W
    band = NA_KH * w
    lane = lax.broadcasted_iota(jnp.int32, (w, LANES), 1)
    low = lane < HEAD_DIM

    def body(it, carry):
        chains = []
        for rr in range(NA_ROWS_PER_STEP):
            r = it * NA_ROWS_PER_STEP + rr
            rs = jnp.clip(r - NA_KH // 2, 0, rows - NA_KH)
            q0 = pl.multiple_of(r * w, w)
            k0 = pl.multiple_of(rs * w, w)
            for pr in range(n_pairs):
                chains.append((r - rs, q0, k0, pr, slice(pr * LANES, (pr + 1) * LANES)))
        scores = []
        for var, q0, k0, pr, sl in chains:
            q2 = q_ref[0, pl.ds(q0, w), sl]
            zero = jnp.zeros_like(q2)
            qq = jnp.concatenate([jnp.where(low, q2, zero), jnp.where(low, zero, q2)], axis=0)
            kk = k_ref[0, pl.ds(k0, band), sl]
            s = lax.dot_general(qq, kk, (((1,), (1,)), ((), ())), preferred_element_type=F32)
            scores.append(s + bias_ref[var, pr])
        probs = []
        for s in scores:
            e = jnp.exp2(s - jnp.max(s, axis=-1, keepdims=True))
            probs.append((e.astype(BF16), jnp.sum(e, axis=-1, keepdims=True)))
        for (var, q0, k0, pr, sl), (e, l) in zip(chains, probs):
            vv = v_ref[0, pl.ds(k0, band), sl]
            pv = jnp.dot(e, vv, preferred_element_type=F32) / l
            o = jnp.where(low, pv[0:w], pv[w:2 * w])
            o_ref[0, pl.ds(q0, w), sl] = o.astype(o_ref.dtype)
        return carry

    lax.fori_loop(0, rows // NA_ROWS_PER_STEP, body, 0)


def _neighbourhood_attention(q, k, v, bias, l):
    b, s, c = q.shape
    rows = s // GRID_W
    assert rows >= NA_KH and rows % NA_ROWS_PER_STEP == 0 and c % LANES == 0
    n_pairs = c // LANES
    seq = pl.BlockSpec((1, s, c), lambda i: (i, 0, 0))
    return pl.pallas_call(
        functools.partial(_na_kernel, rows=rows, n_pairs=n_pairs),
        grid=(b,),
        in_specs=[seq, seq, seq, _layer_spec(bias, l)],
        out_specs=seq,
        out_shape=jax.ShapeDtypeStruct((b, s, c), BF16),
        compiler_params=_params("parallel"),
        name="neighbourhood_attn",
    )(q, k, v, bias)


def _tail_kernel(yf_ref, s_ref, p_ref, hn_ref, pg_ref, yn_ref, x_ref, mod_ref, modp_ref, wo_ref,
                 g_ref, wg_ref, wu_ref, wd_ref, *rest, d_ff, splits):
    if splits is None:
        gf_ref, o_ref = rest[:2]
    else:
        modn_ref, gn_ref, win_ref, o_ref = rest[:4]
        p_refs = rest[4:-3]
    act_ref, xmid_ref, hffn_ref = rest[-3:]
    tm = x_ref.shape[1]
    n_lt = yf_ref.shape[1]
    ts = LRU_TILE

    @pl.when(pl.program_id(0) == 0)
    def _():
        xmid_ref[...] = jnp.zeros_like(xmid_ref)
        hffn_ref[...] = jnp.zeros_like(hffn_ref)

    h_bwd_tail = jnp.concatenate(
        [p_ref[0, k * ts:(k + 1) * ts, :].astype(F32) * hn_ref[0, 0, k:k + 1, :]
         for k in range(tm // ts)], axis=0)
    yl = (_gelu_tanh(pg_ref[0].astype(F32)) * (s_ref[0].astype(F32) + h_bwd_tail)).astype(BF16)
    yf = [jnp.concatenate([yf_ref[0, lt, a * ROW_PITCH:a * ROW_PITCH + GRID_W, :]
                           for a in range(tm // GRID_W)], axis=0).astype(BF16) for lt in range(n_lt)]
    mixed = jnp.concatenate(yf + [yl, yn_ref[0]], axis=1)

    h = hffn_ref[...]
    for j in range(d_ff // FF_CHUNK):
        cs = slice(j * FF_CHUNK, (j + 1) * FF_CHUNK)
        gt = jnp.dot(h, wg_ref[:, cs], preferred_element_type=F32)
        up = jnp.dot(h, wu_ref[:, cs], preferred_element_type=F32)
        act_ref[:, cs] = (gt * jax.nn.sigmoid(gt) * up).astype(BF16)
    f = jnp.dot(act_ref[...], wd_ref[...], preferred_element_type=F32)

    y = jnp.dot(mixed, wo_ref[...], preferred_element_type=F32)
    x = xmid_ref[...] + modp_ref[0, 5:6, :] * f
    if splits is None:
        ms = jnp.mean(x * x, axis=-1, keepdims=True)
        o_ref[0] = x * lax.rsqrt(ms + EPS) * gf_ref[...]
    else:
        o_ref[0] = x
        hn = _norm_mod(x, gn_ref[...], modn_ref[0, 0:1, :], modn_ref[0, 1:2, :]).astype(BF16)

    xm = x_ref[0] + mod_ref[0, 2:3, :] * y
    xmid_ref[...] = xm
    hffn_ref[...] = _norm_mod(xm, g_ref[...], mod_ref[0, 3:4, :], mod_ref[0, 4:5, :]).astype(BF16)

    if splits is not None:
        _emit_in_proj(hn, win_ref, p_refs, splits)


def _layer_tail(y_f, s_lru, p_lru, h_next, p_g, y_n, x, mod, w_out, g_ffn, w_gate, w_up, w_down, l,
                nxt=None, g_final=None, splits=None):
    b, s, d = x.shape
    tm = TOKEN_TILE
    d_ff = w_gate.shape[-1]
    assert d_ff % FF_CHUNK == 0 and tm % GRID_W == 0 and tm % LRU_TILE == 0
    nt = s // tm
    n_tiles = b * nt
    cur = lambda k: jnp.minimum(k, n_tiles - 1)
    prv = lambda k: jnp.maximum(k - 1, 0)
    tok_in = lambda w: pl.BlockSpec((1, tm, w), lambda k: (cur(k) // nt, cur(k) % nt, 0))
    mod_at = lambda layer, sel: pl.BlockSpec((None, 1) + mod.shape[2:],
                                             lambda k: (layer, sel(k) // nt, 0, 0))
    in_specs = [pl.BlockSpec((1, y_f.shape[1], tm // GRID_W * ROW_PITCH, LANES),
                             lambda k: (cur(k) // nt, 0, cur(k) % nt, 0)),
                tok_in(s_lru.shape[-1]), tok_in(p_lru.shape[-1]),
                pl.BlockSpec((1, 1) + h_next.shape[2:], lambda k: (cur(k) // nt, cur(k) % nt, 0, 0)),
                tok_in(p_g.shape[-1]), tok_in(y_n.shape[-1]), tok_in(d),
                mod_at(l, cur), mod_at(l, prv), _layer_spec(w_out, l), _layer_spec(g_ffn, l),
                _layer_spec(w_gate, l), _layer_spec(w_up, l), _layer_spec(w_down, l)]
    args = [y_f, s_lru, p_lru, h_next, p_g, y_n, x, mod, mod, w_out, g_ffn, w_gate, w_up, w_down]
    out_shape = [jax.ShapeDtypeStruct((b, s, d), F32)]
    out_specs = [pl.BlockSpec((1, tm, d), lambda k: (prv(k) // nt, prv(k) % nt, 0))]
    if nxt is None:
        in_specs += [_const_spec(g_final.shape)]
        args += [g_final]
        splits = None
    else:
        g_mix, w_in = nxt
        in_specs += [mod_at(l + 1, prv), _layer_spec(g_mix, l + 1), _layer_spec(w_in, l + 1)]
        args += [mod, g_mix, w_in]
        shapes, specs = _in_proj_out(b, s, tm, splits, lambda k: (prv(k) // nt, prv(k) % nt))
        out_shape += shapes
        out_specs += specs
    return pl.pallas_call(
        functools.partial(_tail_kernel, d_ff=d_ff, splits=splits),
        grid=(n_tiles + 1,),
        in_specs=in_specs,
        out_specs=out_specs,
        out_shape=out_shape,
        scratch_shapes=[pltpu.VMEM((tm, d_ff), BF16),
                        pltpu.VMEM((tm, d), F32),
                        pltpu.VMEM((tm, d), BF16)],
        compiler_params=pltpu.CompilerParams(dimension_semantics=("arbitrary",),
                                             vmem_limit_bytes=TAIL_VMEM_LIMIT),
        name="layer_tail",
    )(*args)


def kernel(x, c, w_ada, b_ada, g_mix, g_ffn, w_in, w_fourier, conv_w, conv_b, lru_w_a, lru_b_a,
           lru_w_x, lru_b_x, lru_lambda, na_rpb, w_out, w_ffn_gate, w_ffn_up, w_ffn_down, g_final):
    b, s, d = x.shape
    depth = w_in.shape[0]
    n_groups = w_fourier.shape[1]
    d_f = n_groups * HEAD_DIM
    d_l = conv_w.shape[-1]
    d_n = na_rpb.shape[1] * HEAD_DIM
    tm = TOKEN_TILE

    mod = _modulation(c, w_ada, b_ada).reshape(depth, b, N_MOD, d)
    tables = _fourier_tables(n_groups)
    splits = _in_proj_splits(d_f, d_l, d_n)
    w_in_b = w_in.astype(BF16)
    w_out_b = w_out.astype(BF16)
    w_gate_b = w_ffn_gate.astype(BF16)
    w_up_b = w_ffn_up.astype(BF16)
    w_down_b = w_ffn_down.astype(BF16)
    g_mix3 = g_mix.reshape(depth, 1, d)
    g_ffn3 = g_ffn.reshape(depth, 1, d)
    w_fourier_b = _block_diag(w_fourier).astype(BF16)
    w_gates = _gate_weights(lru_w_a, lru_b_a, lru_w_x, lru_b_x)
    conv_b3 = conv_b.reshape(depth, 1, d_l)
    na_bias = _na_bias(na_rpb)

    p_f, p_x, p_g, q, k, v = _in_proj(x, mod, g_mix3, w_in_b, 0, splits)
    for l in range(depth):
        y_f = _fourier_mix(p_f, w_fourier_b, l, tables)
        s_lru, p_lru, h_next = _rg_lru(p_x, conv_w, conv_b3, w_gates, lru_lambda, l)
        h_next = jnp.transpose(h_next, (1, 0, 2)).reshape(b, s // tm, tm // LRU_TILE, d_l)
        y_n = _neighbourhood_attention(q, k, v, na_bias, l)
        common = (y_f, s_lru, p_lru, h_next, p_g, y_n, x, mod, w_out_b, g_ffn3,
                  w_gate_b, w_up_b, w_down_b, l)
        if l + 1 < depth:
            x, p_f, p_x, p_g, q, k, v = _layer_tail(*common, nxt=(g_mix3, w_in_b), splits=splits)
        else:
            (x,) = _layer_tail(*common, g_final=g_final.reshape(1, d))
    return x
```

```python
import functools

import numpy as np
import jax
import jax.numpy as jnp
from jax import lax
from jax.experimental import pallas as pl
from jax.experimental.pallas import tpu as pltpu

F32 = jnp.float32
BF16 = jnp.bfloat16

HEAD_DIM = 64
GRID_W = 64
CONV_W = 4
LRU_C = 8.0
NA_KH = 8
NA_KW = 16
N_MOD = 6
EPS = 1e-6
NEG_BIAS = -1e30
LOG2E = 1.4426950408889634
NA_ROWS_PER_STEP = 2

LANES = 128
SUBLANES = 8
VMEM_LIMIT = 56 * 1024 * 1024
TAIL_VMEM_LIMIT = 60 * 1024 * 1024

TOKEN_TILE = 512
FF_CHUNK = 256
LRU_TILE = 128
LRU_GROUP = 8


def _params(*sem):
    return pltpu.CompilerParams(dimension_semantics=sem, vmem_limit_bytes=VMEM_LIMIT)


def _const_spec(shape):
    nd = len(shape)
    return pl.BlockSpec(shape, lambda *_: (0,) * nd, pipeline_mode=pl.Buffered(1))


def _layer_spec(arr, l):
    nd = arr.ndim - 1
    return pl.BlockSpec((None,) + arr.shape[1:], lambda *_: (l,) + (0,) * nd,
                        pipeline_mode=pl.Buffered(1))


def _mod_spec(mod, l):
    return pl.BlockSpec((None, 1) + mod.shape[2:], lambda i, j: (l, i, 0, 0))


def _norm_mod(x, g, shift, scale):
    ms = jnp.mean(x * x, axis=-1, keepdims=True)
    return (x * lax.rsqrt(ms + EPS)) * (g * (1.0 + scale)) + shift


def _gelu_tanh(x):
    return 0.5 * x * (1.0 + jnp.tanh(0.7978845608028654 * (x + 0.044715 * (x * x * x))))


def _mod_kernel(c_ref, w_ref, b_ref, o_ref):
    c = c_ref[...]
    ca = c * jax.nn.sigmoid(c)
    o_ref[0] = jnp.dot(ca, w_ref[0], precision=lax.Precision.HIGHEST,
                       preferred_element_type=F32) + b_ref[0]


def _modulation(c, w_ada, b_ada):
    depth, d, n = w_ada.shape
    b = c.shape[0]
    tn = 1024
    return pl.pallas_call(
        _mod_kernel,
        grid=(depth, n // tn),
        in_specs=[pl.BlockSpec((b, d), lambda l, j: (0, 0)),
                  pl.BlockSpec((1, d, tn), lambda l, j: (l, 0, j)),
                  pl.BlockSpec((1, 1, tn), lambda l, j: (l, 0, j))],
        out_specs=pl.BlockSpec((1, b, tn), lambda l, j: (l, 0, j)),
        out_shape=jax.ShapeDtypeStruct((depth, b, n), F32),
        compiler_params=_params("arbitrary", "arbitrary"),
        name="adaln_mod",
    )(c, w_ada, b_ada.reshape(depth, 1, n))


ROW_PITCH = GRID_W + SUBLANES


def _in_proj_splits(d_f, d_l, d_n):
    o1 = d_f
    o2 = o1 + d_l
    o3 = o2 + d_l
    o4 = o3 + d_n
    o5 = o4 + d_n
    o6 = o5 + d_n
    return ((0, o1, 1.0), (o1, o2, 1.0), (o2, o3, 1.0),
            (o3, o4, HEAD_DIM ** -0.5 * LOG2E), (o4, o5, 1.0), (o5, o6, 1.0))


def _in_proj_out(b, s, tm, splits, tile_of=lambda i, j: (i, j)):
    (lo, hi, _), rest = splits[0], splits[1:]
    rows = s // GRID_W

    def pf_map(*g):
        i, j = tile_of(*g)
        return i, 0, j, 0

    def tok_map(*g):
        i, j = tile_of(*g)
        return i, j, 0

    shapes = [jax.ShapeDtypeStruct((b, (hi - lo) // LANES, rows * ROW_PITCH, LANES), F32)]
    specs = [pl.BlockSpec((1, (hi - lo) // LANES, tm // GRID_W * ROW_PITCH, LANES), pf_map)]
    for lo, hi, _ in rest:
        shapes.append(jax.ShapeDtypeStruct((b, s, hi - lo), BF16))
        specs.append(pl.BlockSpec((1, tm, hi - lo), tok_map))
    return shapes, specs


IN_PROJ_GROUPS = ((1, 2), (3, 4), (5,))


def _emit_in_proj(h, w_ref, out_refs, splits, row0=0):
    pf_ref = out_refs[0]
    lo, hi, _ = splits[0]
    n_rows = h.shape[0]
    p = jnp.dot(h, w_ref[:, lo:hi], preferred_element_type=F32)
    pad = jnp.zeros((ROW_PITCH - GRID_W, LANES), F32)
    for lt in range((hi - lo) // LANES):
        for a in range(n_rows // GRID_W):
            dst = (row0 // GRID_W + a) * ROW_PITCH
            pf_ref[0, lt, dst:dst + GRID_W, :] = p[a * GRID_W:(a + 1) * GRID_W, lt * LANES:(lt + 1) * LANES]
            pf_ref[0, lt, dst + GRID_W:dst + ROW_PITCH, :] = pad
    for group in IN_PROJ_GROUPS:
        g_lo, g_hi = splits[group[0]][0], splits[group[-1]][1]
        p = jnp.dot(h, w_ref[:, g_lo:g_hi], preferred_element_type=F32)
        for idx in group:
            lo, hi, scale = splits[idx]
            part = p[:, lo - g_lo:hi - g_lo]
            if scale != 1.0:
                part = part * scale
            out_refs[idx][0, row0:row0 + n_rows, :] = part.astype(out_refs[idx].dtype)


def _in_proj_kernel(x_ref, mod_ref, g_ref, w_ref, *out_refs, splits):
    h = _norm_mod(x_ref[0], g_ref[...], mod_ref[0, 0:1, :], mod_ref[0, 1:2, :]).astype(BF16)
    _emit_in_proj(h, w_ref, out_refs, splits)


def _in_proj(x, mod, g_mix, w_in, l, splits):
    b, s, d = x.shape
    tm = TOKEN_TILE
    shapes, specs = _in_proj_out(b, s, tm, splits)
    return pl.pallas_call(
        functools.partial(_in_proj_kernel, splits=splits),
        grid=(b, s // tm),
        in_specs=[pl.BlockSpec((1, tm, d), lambda i, j: (i, j, 0)),
                  _mod_spec(mod, l), _layer_spec(g_mix, l), _layer_spec(w_in, l)],
        out_specs=specs,
        out_shape=shapes,
        compiler_params=_params("parallel", "arbitrary"),
        name="in_proj",
    )(x, mod, g_mix, w_in)


FOURIER_SLAB = 2 * GRID_W + SUBLANES
FOURIER_COLS = 8


def _fourier_tables(n_groups):
    n = GRID_W
    idx = np.arange(n)
    ang1 = 2.0 * np.pi * np.outer(idx, idx) / n
    f1 = np.concatenate([np.cos(ang1), np.sin(ang1)], axis=0)
    c = idx[:, None, None]
    dd = idx[None, :, None]
    bb = idx[None, None, :]
    ang2 = 2.0 * np.pi * bb * (n * dd + c) / (n * n)
    gc, gs = np.cos(ang2), np.sin(ang2)
    g2 = np.concatenate([np.concatenate([gc, -gs], axis=2),
                         np.concatenate([gs, gc], axis=2)], axis=1)
    angd = 2.0 * np.pi * np.outer(np.arange(HEAD_DIM), np.arange(HEAD_DIM)) / HEAD_DIM
    eye = np.eye(n_groups)
    cs = np.concatenate([np.kron(eye, np.cos(angd)), -np.kron(eye, np.sin(angd))], axis=0)
    return jnp.asarray(f1, BF16), jnp.asarray(g2, BF16), jnp.asarray(cs, BF16)


def _fourier_kernel(x_ref, f1_ref, g2_ref, cs_ref, w_ref, o_ref, z_ref, y_ref, *, width, scale):
    n = GRID_W
    n_lt = width // LANES
    f1 = f1_ref[...]
    for b in range(n):
        xb = jnp.concatenate([x_ref[0, lt, pl.ds(b, n, stride=ROW_PITCH), :] for lt in range(n_lt)],
                             axis=1).astype(BF16)
        slab = jnp.dot(f1, xb, preferred_element_type=F32)
        for lt in range(n_lt):
            z_ref[lt, b * FOURIER_SLAB:b * FOURIER_SLAB + 2 * n, :] = slab[:, lt * LANES:(lt + 1) * LANES]
    for c in range(n):
        z = jnp.concatenate(
            [jnp.concatenate([z_ref[lt, pl.ds(off + c, n, stride=FOURIER_SLAB), :]
                              for lt in range(n_lt)], axis=1)
             for off in (0, n)], axis=0).astype(BF16)
        y = jnp.dot(g2_ref[c], z, preferred_element_type=F32)
        y_ref[c * n:(c + 1) * n, 0:width] = y[0:n].astype(BF16)
        y_ref[c * n:(c + 1) * n, width:2 * width] = y[n:2 * n].astype(BF16)
    cs = cs_ref[...]
    w = w_ref[...]
    for blk in range(n // FOURIER_COLS):
        rows = slice(blk * FOURIER_COLS * n, (blk + 1) * FOURIER_COLS * n)
        f = jnp.dot(y_ref[rows, :], cs, preferred_element_type=F32) * scale
        out = jnp.dot(f.astype(BF16), w, preferred_element_type=F32)
        for cc in range(FOURIER_COLS):
            c = blk * FOURIER_COLS + cc
            for lt in range(n_lt):
                o_ref[0, lt, pl.ds(c, n, stride=ROW_PITCH), :] = \
                    out[cc * n:(cc + 1) * n, lt * LANES:(lt + 1) * LANES]
    pad = jnp.zeros((ROW_PITCH - n, LANES), F32)
    for lt in range(n_lt):
        for dd in range(n):
            o_ref[0, lt, dd * ROW_PITCH + n:(dd + 1) * ROW_PITCH, :] = pad


def _fourier_mix(p_f, w_blk, l, tables):
    b, n_lt, prow, _ = p_f.shape
    n = GRID_W
    assert prow == n * ROW_PITCH, "sequence DFT is factored as 64 x 64"
    width = n_lt * LANES
    f1, g2, cs = tables
    scale = float(1.0 / np.sqrt(n * n * HEAD_DIM))
    blk = pl.BlockSpec((1, n_lt, prow, LANES), lambda i: (i, 0, 0, 0))
    return pl.pallas_call(
        functools.partial(_fourier_kernel, width=width, scale=scale),
        grid=(b,),
        in_specs=[blk,
                  _const_spec(f1.shape), _const_spec(g2.shape),
                  _const_spec(cs.shape), _layer_spec(w_blk, l)],
        out_specs=blk,
        out_shape=jax.ShapeDtypeStruct(p_f.shape, F32),
        scratch_shapes=[pltpu.VMEM((n_lt, n * FOURIER_SLAB, LANES), F32),
                        pltpu.VMEM((n * n, 2 * width), BF16)],
        compiler_params=_params("parallel"),
        name="fourier_mix",
    )(p_f, f1, g2, cs, w_blk)


LRU_HALO = 2 * SUBLANES
LRU_XSTRIDE = LRU_TILE + 2 * LRU_HALO + SUBLANES
LRU_CONV_LEFT = 2
LRU_CHUNK = 16
LRU_BIAS_ROWS = 2


def _softplus_neg(lam):
    e = jnp.exp(-jnp.abs(lam))
    u = 1.0 + e
    l1p = jnp.where(u == 1.0, e, jnp.log(u) * (e / (u - 1.0)))
    return jnp.maximum(-lam, 0.0) + l1p


def _lru_kernel(x_ref, xp_ref, xn_ref, cw_ref, cb_ref, w_ref, lam_ref,
                s_ref, p_ref, carry_ref,
                xs_ref, xt_ref, af_ref, bf_ref, ab_ref, bb_ref, hs_ref, ps_ref,
                cf_ref, lsum_ref, psum_ref, *, nt):
    ts, grp = LRU_TILE, LRU_GROUP
    n_lt = xs_ref.shape[0]
    c = n_lt * LANES
    lanes = lambda lt: slice(lt * LANES, (lt + 1) * LANES)
    i = pl.program_id(1)
    first = i == 0
    last = i == nt - 1

    @pl.when(first)
    def _():
        cf_ref[...] = jnp.zeros_like(cf_ref)

    for j in range(grp):
        base = j * LRU_XSTRIDE
        for lt in range(n_lt):
            xs_ref[lt, base:base + LRU_HALO, :] = \
                jnp.where(first, 0.0, xp_ref[j, :, lanes(lt)].astype(F32))
            xs_ref[lt, base + LRU_HALO:base + LRU_HALO + ts, :] = x_ref[j, :, lanes(lt)].astype(F32)
            xs_ref[lt, base + LRU_HALO + ts:base + 2 * LRU_HALO + ts, :] = \
                jnp.where(last, 0.0, xn_ref[j, :, lanes(lt)].astype(F32))

    def to_time_major(tt, carry):
        r0 = pl.multiple_of(tt * grp, grp)
        src = pl.ds(LRU_HALO - LRU_CONV_LEFT + tt, grp, stride=LRU_XSTRIDE)
        for lt in range(n_lt):
            xt_ref[pl.ds(r0, grp), lanes(lt)] = xs_ref[lt, src, :]
        return carry

    lax.fori_loop(0, ts + CONV_W - 1, to_time_major, 0, unroll=True)

    cw = cw_ref[...]
    cb = cb_ref[...]
    coef = [(0.5 * LRU_C) * _softplus_neg(lam_ref[d:d + 1, :]) for d in range(2)]
    rows_c = LRU_CHUNK * grp
    bias_cols = jnp.where(lax.broadcasted_iota(jnp.int32, (rows_c, LANES), 1) < LRU_BIAS_ROWS,
                          1.0, 0.0).astype(BF16)

    def gate_chunk(ch, carry):
        r0 = pl.multiple_of(ch * rows_c, rows_c)
        u = cb
        for k in range(CONV_W):
            u = u + cw[k:k + 1, :] * xt_ref[pl.ds(r0 + k * grp, rows_c), :]
        ub = jnp.concatenate([u.astype(BF16), bias_cols], axis=1)
        uh = 0.5 * u
        for d, (a_ref, b_ref) in enumerate(((af_ref, bf_ref), (ab_ref, bb_ref))):
            z = jnp.dot(ub, w_ref[d], preferred_element_type=F32)
            tha = jnp.tanh(z[:, 0:c])
            thx = jnp.tanh(z[:, c:2 * c])
            nla = coef[d] * (tha + 1.0)
            a = jnp.exp2(nla * (-LOG2E))
            m = jnp.tanh(nla) * (1.0 + a * a)
            sq = jnp.where(m > 0.0, m * lax.rsqrt(m), 0.0)
            a_ref[pl.ds(r0, rows_c), :] = a
            b_ref[pl.ds(r0, rows_c), :] = sq * ((thx + 1.0) * uh)
        return carry

    lax.fori_loop(0, ts // LRU_CHUNK, gate_chunk, 0, unroll=True)

    def bwd_step(s, carry):
        h, p = carry
        rows = pl.ds(pl.multiple_of((ts - 1 - s) * grp, grp), grp)
        a = [ab_ref[rows, lanes(lt)] for lt in range(n_lt)]
        h = tuple(a[lt] * h[lt] + bb_ref[rows, lanes(lt)] for lt in range(n_lt))
        p = tuple(a[lt] * p[lt] for lt in range(n_lt))
        for lt in range(n_lt):
            hs_ref[lt, rows, :] = h[lt]
            ps_ref[lt, rows, :] = p[lt]
        return h, p

    zero = jnp.zeros((grp, LANES), F32)
    h, p = lax.fori_loop(0, ts, bwd_step, ((zero,) * n_lt, (zero + 1.0,) * n_lt), unroll=True)
    for lt in range(n_lt):
        lsum_ref[i, :, lanes(lt)] = h[lt]
        psum_ref[i, :, lanes(lt)] = p[lt]

    def fwd_step(t, h):
        rows = pl.ds(pl.multiple_of(t * grp, grp), grp)
        h = tuple(af_ref[rows, lanes(lt)] * h[lt] + bf_ref[rows, lanes(lt)] for lt in range(n_lt))
        for lt in range(n_lt):
            hs_ref[lt, rows, :] = hs_ref[lt, rows, :] + h[lt]
        return h

    h = lax.fori_loop(0, ts, fwd_step, tuple(cf_ref[:, lanes(lt)] for lt in range(n_lt)), unroll=True)
    for lt in range(n_lt):
        cf_ref[:, lanes(lt)] = h[lt]

    pack = 2 * SUBLANES
    for src_ref, o_ref in ((hs_ref, s_ref), (ps_ref, p_ref)):
        for j in range(grp):
            for tb in range(ts // pack):
                for lt in range(n_lt):
                    o_ref[j, tb * pack:(tb + 1) * pack, lanes(lt)] = jnp.concatenate(
                        [src_ref[lt, pl.ds((tb * pack + k * SUBLANES) * grp + j, SUBLANES, stride=grp), :]
                         for k in range(2)], axis=0).astype(o_ref.dtype)

    @pl.when(last)
    def _():
        hn = jnp.zeros((grp, c), F32)
        for t in reversed(range(nt)):
            carry_ref[t] = hn
            hn = lsum_ref[t] + psum_ref[t] * hn


def _gate_weights(w_a, b_a, w_x, b_x):
    w = jnp.concatenate([_block_diag(0.5 * w_a), _block_diag(0.5 * w_x)], axis=-1)
    b = 0.5 * jnp.concatenate([b_a, b_x], axis=-1)
    b_hi = b.astype(BF16).astype(F32)
    extra = jnp.stack([b_hi, b - b_hi], axis=-2)
    extra = jnp.pad(extra, [(0, 0), (0, 0), (0, LANES - LRU_BIAS_ROWS), (0, 0)])
    return jnp.concatenate([w, extra], axis=-2).astype(BF16)


def _rg_lru(p_x, conv_w, conv_b, w_gates, lam, l):
    b, s, c = p_x.shape
    ts, grp, halo = LRU_TILE, LRU_GROUP, LRU_HALO
    assert b % grp == 0 and s % ts == 0
    assert grp == SUBLANES and ts % LRU_CHUNK == 0 and c % LANES == 0
    nt = s // ts
    hb_per_tile = ts // halo
    n_halo = s // halo
    n_lt = c // LANES
    cur = pl.BlockSpec((grp, ts, c), lambda g, i: (g, i, 0))
    prev = pl.BlockSpec((grp, halo, c), lambda g, i: (g, jnp.maximum(i * hb_per_tile - 1, 0), 0))
    nxt = pl.BlockSpec((grp, halo, c),
                       lambda g, i: (g, jnp.minimum((i + 1) * hb_per_tile, n_halo - 1), 0))
    tile_f32 = pltpu.VMEM((ts * grp, c), F32)
    planes_f32 = pltpu.VMEM((n_lt, ts * grp, LANES), F32)
    return pl.pallas_call(
        functools.partial(_lru_kernel, nt=nt),
        grid=(b // grp, nt),
        in_specs=[cur, prev, nxt,
                  _layer_spec(conv_w, l), _layer_spec(conv_b, l), _layer_spec(w_gates, l),
                  _layer_spec(lam, l)],
        out_specs=[cur, cur, pl.BlockSpec((nt, grp, c), lambda g, i: (0, g, 0))],
        out_shape=[jax.ShapeDtypeStruct((b, s, c), BF16), jax.ShapeDtypeStruct((b, s, c), BF16),
                   jax.ShapeDtypeStruct((nt, b, c), F32)],
        scratch_shapes=[pltpu.VMEM((n_lt, grp * LRU_XSTRIDE, LANES), F32),
                        pltpu.VMEM(((ts + CONV_W - 1) * grp, c), F32),
                        tile_f32, tile_f32, tile_f32, tile_f32,
                        planes_f32, planes_f32,
                        pltpu.VMEM((grp, c), F32),
                        pltpu.VMEM((nt, grp, c), F32),
                        pltpu.VMEM((nt, grp, c), F32)],
        compiler_params=_params("parallel", "arbitrary"),
        name="rg_lru",
    )(p_x, p_x, p_x, conv_w, conv_b, w_gates, lam)


def _block_diag(w):
    h, dh = w.shape[-3], w.shape[-1]
    lead = [(0, 0)] * (w.ndim - 2)
    rows = [jnp.pad(w[..., k, :, :], lead + [(k * dh, (h - 1 - k) * dh)]) for k in range(h)]
    return jnp.concatenate(rows, axis=-2)


def _na_bias(rpb):
    n_l, h = rpb.shape[:2]
    w, kh, kw = GRID_W, NA_KH, NA_KW
    qc = np.arange(w)[:, None]
    kc = np.arange(w)[None, :]
    cs = np.clip(qc - kw // 2, 0, w - kw)
    inwin = (kc >= cs) & (kc < cs + kw)
    sel_c = (inwin[:, :, None]
             & (np.arange(2 * kw - 1)[None, None, :] == (kc - qc + kw - 1)[:, :, None])).astype(np.float32)
    var = np.arange(kh)[:, None]
    ii = np.arange(kh)[None, :]
    sel_r = (np.arange(2 * kh - 1)[None, None, :] == (ii - var + kh - 1)[:, :, None]).astype(np.float32)
    t = jnp.einsum('lhrc,vir,qkc->lvhqik', rpb * LOG2E, sel_r, sel_c,
                   precision=lax.Precision.HIGHEST)
    t = t + np.where(inwin, 0.0, NEG_BIAS).astype(np.float32)[:, None, :]
    return t.reshape(n_l, kh, h // 2, 2 * w, kh * w)


def _na_kernel(q_ref, k_ref, v_ref, bias_ref, o_ref, *, rows, n_pairs):
    w = GRID_W
    band = NA_KH * w
    lane = lax.broadcasted_iota(jnp.int32, (w, LANES), 1)
    low = lane < HEAD_DIM

    def body(it, carry):
        chains = []
        for rr in range(NA_ROWS_PER_STEP):
            r = it * NA_ROWS_PER_STEP + rr
            rs = jnp.clip(r - NA_KH // 2, 0, rows - NA_KH)
            q0 = pl.multiple_of(r * w, w)
            k0 = pl.multiple_of(rs * w, w)
            for pr in range(n_pairs):
                chains.append((r - rs, q0, k0, pr, slice(pr * LANES, (pr + 1) * LANES)))
        scores = []
        for var, q0, k0, pr, sl in chains:
            q2 = q_ref[0, pl.ds(q0, w), sl]
            zero = jnp.zeros_like(q2)
            qq = jnp.concatenate([jnp.where(low, q2, zero), jnp.where(low, zero, q2)], axis=0)
            kk = k_ref[0, pl.ds(k0, band), sl]
            s = lax.dot_general(qq, kk, (((1,), (1,)), ((), ())), preferred_element_type=F32)
            scores.append(s + bias_ref[var, pr])
        probs = []
        for s in scores:
            e = jnp.exp2(s - jnp.max(s, axis=-1, keepdims=True))
            probs.append((e.astype(BF16), jnp.sum(e, axis=-1, keepdims=True)))
        for (var, q0, k0, pr, sl), (e, l) in zip(chains, probs):
            vv = v_ref[0, pl.ds(k0, band), sl]
            pv = jnp.dot(e, vv, preferred_element_type=F32) / l
            o = jnp.where(low, pv[0:w], pv[w:2 * w])
            o_ref[0, pl.ds(q0, w), sl] = o.astype(o_ref.dtype)
        return carry

    lax.fori_loop(0, rows // NA_ROWS_PER_STEP, body, 0)


def _neighbourhood_attention(q, k, v, bias, l):
    b, s, c = q.shape
    rows = s // GRID_W
    assert rows >= NA_KH and rows % NA_ROWS_PER_STEP == 0 and c % LANES == 0
    n_pairs = c // LANES
    seq = pl.BlockSpec((1, s, c), lambda i: (i, 0, 0))
    return pl.pallas_call(
        functools.partial(_na_kernel, rows=rows, n_pairs=n_pairs),
        grid=(b,),
        in_specs=[seq, seq, seq, _layer_spec(bias, l)],
        out_specs=seq,
        out_shape=jax.ShapeDtypeStruct((b, s, c), BF16),
        compiler_params=_params("parallel"),
        name="neighbourhood_attn",
    )(q, k, v, bias)


def _tail_kernel(yf_ref, s_ref, p_ref, hn_ref, pg_ref, yn_ref, x_ref, mod_ref, modp_ref, wo_ref,
                 g_ref, wg_ref, wu_ref, wd_ref, *rest, d_ff, splits):
    if splits is None:
        gf_ref, o_ref = rest[:2]
    else:
        modn_ref, gn_ref, win_ref, o_ref = rest[:4]
        p_refs = rest[4:-3]
    act_ref, xmid_ref, hffn_ref = rest[-3:]
    tm = x_ref.shape[1]
    n_lt = yf_ref.shape[1]
    ts = LRU_TILE

    @pl.when(pl.program_id(0) == 0)
    def _():
        xmid_ref[...] = jnp.zeros_like(xmid_ref)
        hffn_ref[...] = jnp.zeros_like(hffn_ref)

    h_bwd_tail = jnp.concatenate(
        [p_ref[0, k * ts:(k + 1) * ts, :].astype(F32) * hn_ref[0, 0, k:k + 1, :]
         for k in range(tm // ts)], axis=0)
    yl = (_gelu_tanh(pg_ref[0].astype(F32)) * (s_ref[0].astype(F32) + h_bwd_tail)).astype(BF16)
    yf = [jnp.concatenate([yf_ref[0, lt, a * ROW_PITCH:a * ROW_PITCH + GRID_W, :]
                           for a in range(tm // GRID_W)], axis=0).astype(BF16) for lt in range(n_lt)]
    mixed = jnp.concatenate(yf + [yl, yn_ref[0]], axis=1)

    h = hffn_ref[...]
    for j in range(d_ff // FF_CHUNK):
        cs = slice(j * FF_CHUNK, (j + 1) * FF_CHUNK)
        gt = jnp.dot(h, wg_ref[:, cs], preferred_element_type=F32)
        up = jnp.dot(h, wu_ref[:, cs], preferred_element_type=F32)
        act_ref[:, cs] = (gt * jax.nn.sigmoid(gt) * up).astype(BF16)
    f = jnp.dot(act_ref[...], wd_ref[...], preferred_element_type=F32)

    y = jnp.dot(mixed, wo_ref[...], preferred_element_type=F32)
    x = xmid_ref[...] + modp_ref[0, 5:6, :] * f
    if splits is None:
        ms = jnp.mean(x * x, axis=-1, keepdims=True)
        o_ref[0] = x * lax.rsqrt(ms + EPS) * gf_ref[...]
    else:
        o_ref[0] = x
        hn = _norm_mod(x, gn_ref[...], modn_ref[0, 0:1, :], modn_ref[0, 1:2, :]).astype(BF16)

    xm = x_ref[0] + mod_ref[0, 2:3, :] * y
    xmid_ref[...] = xm
    hffn_ref[...] = _norm_mod(xm, g_ref[...], mod_ref[0, 3:4, :], mod_ref[0, 4:5, :]).astype(BF16)

    if splits is not None:
        _emit_in_proj(hn, win_ref, p_refs, splits)


def _layer_tail(y_f, s_lru, p_lru, h_next, p_g, y_n, x, mod, w_out, g_ffn, w_gate, w_up, w_down, l,
                nxt=None, g_final=None, splits=None):
    b, s, d = x.shape
    tm = TOKEN_TILE
    d_ff = w_gate.shape[-1]
    assert d_ff % FF_CHUNK == 0 and tm % GRID_W == 0 and tm % LRU_TILE == 0
    nt = s // tm
    n_tiles = b * nt
    cur = lambda k: jnp.minimum(k, n_tiles - 1)
    prv = lambda k: jnp.maximum(k - 1, 0)
    tok_in = lambda w: pl.BlockSpec((1, tm, w), lambda k: (cur(k) // nt, cur(k) % nt, 0))
    mod_at = lambda layer, sel: pl.BlockSpec((None, 1) + mod.shape[2:],
                                             lambda k: (layer, sel(k) // nt, 0, 0))
    in_specs = [pl.BlockSpec((1, y_f.shape[1], tm // GRID_W * ROW_PITCH, LANES),
                             lambda k: (cur(k) // nt, 0, cur(k) % nt, 0)),
                tok_in(s_lru.shape[-1]), tok_in(p_lru.shape[-1]),
                pl.BlockSpec((1, 1) + h_next.shape[2:], lambda k: (cur(k) // nt, cur(k) % nt, 0, 0)),
                tok_in(p_g.shape[-1]), tok_in(y_n.shape[-1]), tok_in(d),
                mod_at(l, cur), mod_at(l, prv), _layer_spec(w_out, l), _layer_spec(g_ffn, l),
                _layer_spec(w_gate, l), _layer_spec(w_up, l), _layer_spec(w_down, l)]
    args = [y_f, s_lru, p_lru, h_next, p_g, y_n, x, mod, mod, w_out, g_ffn, w_gate, w_up, w_down]
    out_shape = [jax.ShapeDtypeStruct((b, s, d), F32)]
    out_specs = [pl.BlockSpec((1, tm, d), lambda k: (prv(k) // nt, prv(k) % nt, 0))]
    if nxt is None:
        in_specs += [_const_spec(g_final.shape)]
        args += [g_final]
        splits = None
    else:
        g_mix, w_in = nxt
        in_specs += [mod_at(l + 1, prv), _layer_spec(g_mix, l + 1), _layer_spec(w_in, l + 1)]
        args += [mod, g_mix, w_in]
        shapes, specs = _in_proj_out(b, s, tm, splits, lambda k: (prv(k) // nt, prv(k) % nt))
        out_shape += shapes
        out_specs += specs
    return pl.pallas_call(
        functools.partial(_tail_kernel, d_ff=d_ff, splits=splits),
        grid=(n_tiles + 1,),
        in_specs=in_specs,
        out_specs=out_specs,
        out_shape=out_shape,
        scratch_shapes=[pltpu.VMEM((tm, d_ff), BF16),
                        pltpu.VMEM((tm, d), F32),
                        pltpu.VMEM((tm, d), BF16)],
        compiler_params=pltpu.CompilerParams(dimension_semantics=("arbitrary",),
                                             vmem_limit_bytes=TAIL_VMEM_LIMIT),
        name="layer_tail",
    )(*args)


def kernel(x, c, w_ada, b_ada, g_mix, g_ffn, w_in, w_fourier, conv_w, conv_b, lru_w_a, lru_b_a,
           lru_w_x, lru_b_x, lru_lambda, na_rpb, w_out, w_ffn_gate, w_ffn_up, w_ffn_down, g_final):
    b, s, d = x.shape
    depth = w_in.shape[0]
    n_groups = w_fourier.shape[1]
    d_f = n_groups * HEAD_DIM
    d_l = conv_w.shape[-1]
    d_n = na_rpb.shape[1] * HEAD_DIM
    tm = TOKEN_TILE

    mod = _modulation(c, w_ada, b_ada).reshape(depth, b, N_MOD, d)
    tables = _fourier_tables(n_groups)
    splits = _in_proj_splits(d_f, d_l, d_n)
    w_in_b = w_in.astype(BF16)
    w_out_b = w_out.astype(BF16)
    w_gate_b = w_ffn_gate.astype(BF16)
    w_up_b = w_ffn_up.astype(BF16)
    w_down_b = w_ffn_down.astype(BF16)
    g_mix3 = g_mix.reshape(depth, 1, d)
    g_ffn3 = g_ffn.reshape(depth, 1, d)
    w_fourier_b = _block_diag(w_fourier).astype(BF16)
    w_gates = _gate_weights(lru_w_a, lru_b_a, lru_w_x, lru_b_x)
    conv_b3 = conv_b.reshape(depth, 1, d_l)
    na_bias = _na_bias(na_rpb)

    p_f, p_x, p_g, q, k, v = _in_proj(x, mod, g_mix3, w_in_b, 0, splits)
    for l in range(depth):
        y_f = _fourier_mix(p_f, w_fourier_b, l, tables)
        s_lru, p_lru, h_next = _rg_lru(p_x, conv_w, conv_b3, w_gates, lru_lambda, l)
        h_next = jnp.transpose(h_next, (1, 0, 2)).reshape(b, s // tm, tm // LRU_TILE, d_l)
        y_n = _neighbourhood_attention(q, k, v, na_bias, l)
        common = (y_f, s_lru, p_lru, h_next, p_g, y_n, x, mod, w_out_b, g_ffn3,
                  w_gate_b, w_up_b, w_down_b, l)
        if l + 1 < depth:
            x, p_f, p_x, p_g, q, k, v = _layer_tail(*common, nxt=(g_mix3, w_in_b), splits=splits)
        else:
            (x,) = _layer_tail(*common, g_final=g_final.reshape(1, d))
    return x
```

```python
import functools

import numpy as np
import jax
import jax.numpy as jnp
from jax import lax
from jax.experimental import pallas as pl
from jax.experimental.pallas import tpu as pltpu

F32 = jnp.float32
BF16 = jnp.bfloat16

HEAD_DIM = 64
GRID_W = 64
CONV_W = 4
LRU_C = 8.0
NA_KH = 8
NA_KW = 16
N_MOD = 6
EPS = 1e-6
NEG_BIAS = -1e30
LOG2E = 1.4426950408889634
NA_ROWS_PER_STEP = 2

LANES = 128
SUBLANES = 8
VMEM_LIMIT = 56 * 1024 * 1024
TAIL_VMEM_LIMIT = 60 * 1024 * 1024

TOKEN_TILE = 512
FF_CHUNK = 256
LRU_TILE = 256
LRU_GROUP = 8


def _params(*sem):
    return pltpu.CompilerParams(dimension_semantics=sem, vmem_limit_bytes=VMEM_LIMIT)


def _const_spec(shape):
    nd = len(shape)
    return pl.BlockSpec(shape, lambda *_: (0,) * nd, pipeline_mode=pl.Buffered(1))


def _layer_spec(arr, l):
    nd = arr.ndim - 1
    return pl.BlockSpec((None,) + arr.shape[1:], lambda *_: (l,) + (0,) * nd,
                        pipeline_mode=pl.Buffered(1))


def _mod_spec(mod, l):
    return pl.BlockSpec((None, 1) + mod.shape[2:], lambda i, j: (l, i, 0, 0))


def _norm_mod(x, g, shift, scale):
    ms = jnp.mean(x * x, axis=-1, keepdims=True)
    return (x * lax.rsqrt(ms + EPS)) * (g * (1.0 + scale)) + shift


def _gelu_tanh(x):
    return 0.5 * x * (1.0 + jnp.tanh(0.7978845608028654 * (x + 0.044715 * (x * x * x))))


def _mod_kernel(c_ref, w_ref, b_ref, o_ref):
    c = c_ref[...]
    ca = c * jax.nn.sigmoid(c)
    o_ref[0] = jnp.dot(ca, w_ref[0], precision=lax.Precision.HIGHEST,
                       preferred_element_type=F32) + b_ref[0]


def _modulation(c, w_ada, b_ada):
    depth, d, n = w_ada.shape
    b = c.shape[0]
    tn = 1024
    return pl.pallas_call(
        _mod_kernel,
        grid=(depth, n // tn),
        in_specs=[pl.BlockSpec((b, d), lambda l, j: (0, 0)),
                  pl.BlockSpec((1, d, tn), lambda l, j: (l, 0, j)),
                  pl.BlockSpec((1, 1, tn), lambda l, j: (l, 0, j))],
        out_specs=pl.BlockSpec((1, b, tn), lambda l, j: (l, 0, j)),
        out_shape=jax.ShapeDtypeStruct((depth, b, n), F32),
        compiler_params=_params("arbitrary", "arbitrary"),
        name="adaln_mod",
    )(c, w_ada, b_ada.reshape(depth, 1, n))


ROW_PITCH = GRID_W + SUBLANES


def _in_proj_splits(d_f, d_l, d_n):
    o1 = d_f
    o2 = o1 + d_l
    o3 = o2 + d_l
    o4 = o3 + d_n
    o5 = o4 + d_n
    o6 = o5 + d_n
    return ((0, o1, 1.0), (o1, o2, 1.0), (o2, o3, 1.0),
            (o3, o4, HEAD_DIM ** -0.5 * LOG2E), (o4, o5, 1.0), (o5, o6, 1.0))


def _in_proj_out(b, s, tm, splits, tile_of=lambda i, j: (i, j)):
    (lo, hi, _), rest = splits[0], splits[1:]
    rows = s // GRID_W

    def pf_map(*g):
        i, j = tile_of(*g)
        return i, 0, j, 0

    def tok_map(*g):
        i, j = tile_of(*g)
        return i, j, 0

    shapes = [jax.ShapeDtypeStruct((b, (hi - lo) // LANES, rows * ROW_PITCH, LANES), F32)]
    specs = [pl.BlockSpec((1, (hi - lo) // LANES, tm // GRID_W * ROW_PITCH, LANES), pf_map)]
    for lo, hi, _ in rest:
        shapes.append(jax.ShapeDtypeStruct((b, s, hi - lo), BF16))
        specs.append(pl.BlockSpec((1, tm, hi - lo), tok_map))
    return shapes, specs


IN_PROJ_GROUPS = ((1, 2), (3, 4), (5,))


def _emit_in_proj(h, w_ref, out_refs, splits, row0=0):
    pf_ref = out_refs[0]
    lo, hi, _ = splits[0]
    n_rows = h.shape[0]
    p = jnp.dot(h, w_ref[:, lo:hi], preferred_element_type=F32)
    pad = jnp.zeros((ROW_PITCH - GRID_W, LANES), F32)
    for lt in range((hi - lo) // LANES):
        for a in range(n_rows // GRID_W):
            dst = (row0 // GRID_W + a) * ROW_PITCH
            pf_ref[0, lt, dst:dst + GRID_W, :] = p[a * GRID_W:(a + 1) * GRID_W, lt * LANES:(lt + 1) * LANES]
            pf_ref[0, lt, dst + GRID_W:dst + ROW_PITCH, :] = pad
    for group in IN_PROJ_GROUPS:
        g_lo, g_hi = splits[group[0]][0], splits[group[-1]][1]
        p = jnp.dot(h, w_ref[:, g_lo:g_hi], preferred_element_type=F32)
        for idx in group:
            lo, hi, scale = splits[idx]
            part = p[:, lo - g_lo:hi - g_lo]
            if scale != 1.0:
                part = part * scale
            out_refs[idx][0, row0:row0 + n_rows, :] = part.astype(out_refs[idx].dtype)


def _in_proj_kernel(x_ref, mod_ref, g_ref, w_ref, *out_refs, splits):
    h = _norm_mod(x_ref[0], g_ref[...], mod_ref[0, 0:1, :], mod_ref[0, 1:2, :]).astype(BF16)
    _emit_in_proj(h, w_ref, out_refs, splits)


def _in_proj(x, mod, g_mix, w_in, l, splits):
    b, s, d = x.shape
    tm = TOKEN_TILE
    shapes, specs = _in_proj_out(b, s, tm, splits)
    return pl.pallas_call(
        functools.partial(_in_proj_kernel, splits=splits),
        grid=(b, s // tm),
        in_specs=[pl.BlockSpec((1, tm, d), lambda i, j: (i, j, 0)),
                  _mod_spec(mod, l), _layer_spec(g_mix, l), _layer_spec(w_in, l)],
        out_specs=specs,
        out_shape=shapes,
        compiler_params=_params("parallel", "arbitrary"),
        name="in_proj",
    )(x, mod, g_mix, w_in)


FOURIER_SLAB = 2 * GRID_W + SUBLANES
FOURIER_COLS = 8


def _fourier_tables(n_groups):
    n = GRID_W
    idx = np.arange(n)
    ang1 = 2.0 * np.pi * np.outer(idx, idx) / n
    f1 = np.concatenate([np.cos(ang1), np.sin(ang1)], axis=0)
    c = idx[:, None, None]
    dd = idx[None, :, None]
    bb = idx[None, None, :]
    ang2 = 2.0 * np.pi * bb * (n * dd + c) / (n * n)
    gc, gs = np.cos(ang2), np.sin(ang2)
    g2 = np.concatenate([np.concatenate([gc, -gs], axis=2),
                         np.concatenate([gs, gc], axis=2)], axis=1)
    angd = 2.0 * np.pi * np.outer(np.arange(HEAD_DIM), np.arange(HEAD_DIM)) / HEAD_DIM
    eye = np.eye(n_groups)
    cs = np.concatenate([np.kron(eye, np.cos(angd)), -np.kron(eye, np.sin(angd))], axis=0)
    return jnp.asarray(f1, BF16), jnp.asarray(g2, BF16), jnp.asarray(cs, BF16)


def _fourier_kernel(x_ref, f1_ref, g2_ref, cs_ref, w_ref, o_ref, z_ref, y_ref, *, width, scale):
    n = GRID_W
    n_lt = width // LANES
    f1 = f1_ref[...]
    for b in range(n):
        xb = jnp.concatenate([x_ref[0, lt, pl.ds(b, n, stride=ROW_PITCH), :] for lt in range(n_lt)],
                             axis=1).astype(BF16)
        slab = jnp.dot(f1, xb, preferred_element_type=F32)
        for lt in range(n_lt):
            z_ref[lt, b * FOURIER_SLAB:b * FOURIER_SLAB + 2 * n, :] = slab[:, lt * LANES:(lt + 1) * LANES]
    for c in range(n):
        z = jnp.concatenate(
            [jnp.concatenate([z_ref[lt, pl.ds(off + c, n, stride=FOURIER_SLAB), :]
                              for lt in range(n_lt)], axis=1)
             for off in (0, n)], axis=0).astype(BF16)
        y = jnp.dot(g2_ref[c], z, preferred_element_type=F32)
        y_ref[c * n:(c + 1) * n, 0:width] = y[0:n].astype(BF16)
        y_ref[c * n:(c + 1) * n, width:2 * width] = y[n:2 * n].astype(BF16)
    cs = cs_ref[...]
    w = w_ref[...]
    for blk in range(n // FOURIER_COLS):
        rows = slice(blk * FOURIER_COLS * n, (blk + 1) * FOURIER_COLS * n)
        f = jnp.dot(y_ref[rows, :], cs, preferred_element_type=F32) * scale
        out = jnp.dot(f.astype(BF16), w, preferred_element_type=F32)
        for cc in range(FOURIER_COLS):
            c = blk * FOURIER_COLS + cc
            for lt in range(n_lt):
                o_ref[0, lt, pl.ds(c, n, stride=ROW_PITCH), :] = \
                    out[cc * n:(cc + 1) * n, lt * LANES:(lt + 1) * LANES]
    pad = jnp.zeros((ROW_PITCH - n, LANES), F32)
    for lt in range(n_lt):
        for dd in range(n):
            o_ref[0, lt, dd * ROW_PITCH + n:(dd + 1) * ROW_PITCH, :] = pad


def _fourier_mix(p_f, w_blk, l, tables):
    b, n_lt, prow, _ = p_f.shape
    n = GRID_W
    assert prow == n * ROW_PITCH, "sequence DFT is factored as 64 x 64"
    width = n_lt * LANES
    f1, g2, cs = tables
    scale = float(1.0 / np.sqrt(n * n * HEAD_DIM))
    blk = pl.BlockSpec((1, n_lt, prow, LANES), lambda i: (i, 0, 0, 0))
    return pl.pallas_call(
        functools.partial(_fourier_kernel, width=width, scale=scale),
        grid=(b,),
        in_specs=[blk,
                  _const_spec(f1.shape), _const_spec(g2.shape),
                  _const_spec(cs.shape), _layer_spec(w_blk, l)],
        out_specs=blk,
        out_shape=jax.ShapeDtypeStruct(p_f.shape, F32),
        scratch_shapes=[pltpu.VMEM((n_lt, n * FOURIER_SLAB, LANES), F32),
                        pltpu.VMEM((n * n, 2 * width), BF16)],
        compiler_params=_params("parallel"),
        name="fourier_mix",
    )(p_f, f1, g2, cs, w_blk)


LRU_HALO = 2 * SUBLANES
LRU_XSTRIDE = LRU_TILE + 2 * LRU_HALO + SUBLANES
LRU_CONV_LEFT = 2
LRU_CHUNK = 16
LRU_BIAS_ROWS = 2


def _softplus_neg(lam):
    e = jnp.exp(-jnp.abs(lam))
    u = 1.0 + e
    l1p = jnp.where(u == 1.0, e, jnp.log(u) * (e / (u - 1.0)))
    return jnp.maximum(-lam, 0.0) + l1p


def _lru_kernel(x_ref, xp_ref, xn_ref, cw_ref, cb_ref, w_ref, lam_ref,
                s_ref, p_ref, carry_ref,
                xs_ref, xt_ref, af_ref, bf_ref, ab_ref, bb_ref, hs_ref, ps_ref,
                cf_ref, lsum_ref, psum_ref, *, nt):
    ts, grp = LRU_TILE, LRU_GROUP
    n_lt = xs_ref.shape[0]
    c = n_lt * LANES
    lanes = lambda lt: slice(lt * LANES, (lt + 1) * LANES)
    i = pl.program_id(1)
    first = i == 0
    last = i == nt - 1

    @pl.when(first)
    def _():
        cf_ref[...] = jnp.zeros_like(cf_ref)

    for j in range(grp):
        base = j * LRU_XSTRIDE
        for lt in range(n_lt):
            xs_ref[lt, base:base + LRU_HALO, :] = \
                jnp.where(first, 0.0, xp_ref[j, :, lanes(lt)].astype(F32))
            xs_ref[lt, base + LRU_HALO:base + LRU_HALO + ts, :] = x_ref[j, :, lanes(lt)].astype(F32)
            xs_ref[lt, base + LRU_HALO + ts:base + 2 * LRU_HALO + ts, :] = \
                jnp.where(last, 0.0, xn_ref[j, :, lanes(lt)].astype(F32))

    def to_time_major(tt, carry):
        r0 = pl.multiple_of(tt * grp, grp)
        src = pl.ds(LRU_HALO - LRU_CONV_LEFT + tt, grp, stride=LRU_XSTRIDE)
        for lt in range(n_lt):
            xt_ref[pl.ds(r0, grp), lanes(lt)] = xs_ref[lt, src, :]
        return carry

    lax.fori_loop(0, ts + CONV_W - 1, to_time_major, 0, unroll=True)

    cw = cw_ref[...]
    cb = cb_ref[...]
    coef = [(0.5 * LRU_C) * _softplus_neg(lam_ref[d:d + 1, :]) for d in range(2)]
    rows_c = LRU_CHUNK * grp
    bias_cols = jnp.where(lax.broadcasted_iota(jnp.int32, (rows_c, LANES), 1) < LRU_BIAS_ROWS,
                          1.0, 0.0).astype(BF16)

    def gate_chunk(ch, carry):
        r0 = pl.multiple_of(ch * rows_c, rows_c)
        u = cb
        for k in range(CONV_W):
            u = u + cw[k:k + 1, :] * xt_ref[pl.ds(r0 + k * grp, rows_c), :]
        ub = jnp.concatenate([u.astype(BF16), bias_cols], axis=1)
        uh = 0.5 * u
        for d, (a_ref, b_ref) in enumerate(((af_ref, bf_ref), (ab_ref, bb_ref))):
            z = jnp.dot(ub, w_ref[d], preferred_element_type=F32)
            tha = jnp.tanh(z[:, 0:c])
            thx = jnp.tanh(z[:, c:2 * c])
            nla = coef[d] * (tha + 1.0)
            a = jnp.exp2(nla * (-LOG2E))
            m = jnp.tanh(nla) * (1.0 + a * a)
            sq = jnp.where(m > 0.0, m * lax.rsqrt(m), 0.0)
            a_ref[pl.ds(r0, rows_c), :] = a
            b_ref[pl.ds(r0, rows_c), :] = sq * ((thx + 1.0) * uh)
        return carry

    lax.fori_loop(0, ts // LRU_CHUNK, gate_chunk, 0, unroll=True)

    def bwd_step(s, carry):
        h, p = carry
        rows = pl.ds(pl.multiple_of((ts - 1 - s) * grp, grp), grp)
        a = [ab_ref[rows, lanes(lt)] for lt in range(n_lt)]
        h = tuple(a[lt] * h[lt] + bb_ref[rows, lanes(lt)] for lt in range(n_lt))
        p = tuple(a[lt] * p[lt] for lt in range(n_lt))
        for lt in range(n_lt):
            hs_ref[lt, rows, :] = h[lt]
            ps_ref[lt, rows, :] = p[lt]
        return h, p

    zero = jnp.zeros((grp, LANES), F32)
    h, p = lax.fori_loop(0, ts, bwd_step, ((zero,) * n_lt, (zero + 1.0,) * n_lt), unroll=True)
    for lt in range(n_lt):
        lsum_ref[i, :, lanes(lt)] = h[lt]
        psum_ref[i, :, lanes(lt)] = p[lt]

    def fwd_step(t, h):
        rows = pl.ds(pl.multiple_of(t * grp, grp), grp)
        h = tuple(af_ref[rows, lanes(lt)] * h[lt] + bf_ref[rows, lanes(lt)] for lt in range(n_lt))
        for lt in range(n_lt):
            hs_ref[lt, rows, :] = hs_ref[lt, rows, :] + h[lt]
        return h

    h = lax.fori_loop(0, ts, fwd_step, tuple(cf_ref[:, lanes(lt)] for lt in range(n_lt)), unroll=True)
    for lt in range(n_lt):
        cf_ref[:, lanes(lt)] = h[lt]

    pack = 2 * SUBLANES
    for src_ref, o_ref in ((hs_ref, s_ref), (ps_ref, p_ref)):
        for j in range(grp):
            for tb in range(ts // pack):
                for lt in range(n_lt):
                    o_ref[j, tb * pack:(tb + 1) * pack, lanes(lt)] = jnp.concatenate(
                        [src_ref[lt, pl.ds((tb * pack + k * SUBLANES) * grp + j, SUBLANES, stride=grp), :]
                         for k in range(2)], axis=0).astype(o_ref.dtype)

    @pl.when(last)
    def _():
        hn = jnp.zeros((grp, c), F32)
        for t in reversed(range(nt)):
            carry_ref[t] = hn
            hn = lsum_ref[t] + psum_ref[t] * hn


def _gate_weights(w_a, b_a, w_x, b_x):
    w = jnp.concatenate([_block_diag(0.5 * w_a), _block_diag(0.5 * w_x)], axis=-1)
    b = 0.5 * jnp.concatenate([b_a, b_x], axis=-1)
    b_hi = b.astype(BF16).astype(F32)
    extra = jnp.stack([b_hi, b - b_hi], axis=-2)
    extra = jnp.pad(extra, [(0, 0), (0, 0), (0, LANES - LRU_BIAS_ROWS), (0, 0)])
    return jnp.concatenate([w, extra], axis=-2).astype(BF16)


def _rg_lru(p_x, conv_w, conv_b, w_gates, lam, l):
    b, s, c = p_x.shape
    ts, grp, halo = LRU_TILE, LRU_GROUP, LRU_HALO
    assert b % grp == 0 and s % ts == 0
    assert grp == SUBLANES and ts % LRU_CHUNK == 0 and c % LANES == 0
    nt = s // ts
    hb_per_tile = ts // halo
    n_halo = s // halo
    n_lt = c // LANES
    cur = pl.BlockSpec((grp, ts, c), lambda g, i: (g, i, 0))
    prev = pl.BlockSpec((grp, halo, c), lambda g, i: (g, jnp.maximum(i * hb_per_tile - 1, 0), 0))
    nxt = pl.BlockSpec((grp, halo, c),
                       lambda g, i: (g, jnp.minimum((i + 1) * hb_per_tile, n_halo - 1), 0))
    tile_f32 = pltpu.VMEM((ts * grp, c), F32)
    planes_f32 = pltpu.VMEM((n_lt, ts * grp, LANES), F32)
    return pl.pallas_call(
        functools.partial(_lru_kernel, nt=nt),
        grid=(b // grp, nt),
        in_specs=[cur, prev, nxt,
                  _layer_spec(conv_w, l), _layer_spec(conv_b, l), _layer_spec(w_gates, l),
                  _layer_spec(lam, l)],
        out_specs=[cur, cur, pl.BlockSpec((nt, grp, c), lambda g, i: (0, g, 0))],
        out_shape=[jax.ShapeDtypeStruct((b, s, c), BF16), jax.ShapeDtypeStruct((b, s, c), BF16),
                   jax.ShapeDtypeStruct((nt, b, c), F32)],
        scratch_shapes=[pltpu.VMEM((n_lt, grp * LRU_XSTRIDE, LANES), F32),
                        pltpu.VMEM(((ts + CONV_W - 1) * grp, c), F32),
                        tile_f32, tile_f32, tile_f32, tile_f32,
                        planes_f32, planes_f32,
                        pltpu.VMEM((grp, c), F32),
                        pltpu.VMEM((nt, grp, c), F32),
                        pltpu.VMEM((nt, grp, c), F32)],
        compiler_params=_params("parallel", "arbitrary"),
        name="rg_lru",
    )(p_x, p_x, p_x, conv_w, conv_b, w_gates, lam)


def _block_diag(w):
    h, dh = w.shape[-3], w.shape[-1]
    lead = [(0, 0)] * (w.ndim - 2)
    rows = [jnp.pad(w[..., k, :, :], lead + [(k * dh, (h - 1 - k) * dh)]) for k in range(h)]
    return jnp.concatenate(rows, axis=-2)


def _na_bias(rpb):
    n_l, h = rpb.shape[:2]
    w, kh, kw = GRID_W, NA_KH, NA_KW
    qc = np.arange(w)[:, None]
    kc = np.arange(w)[None, :]
    cs = np.clip(qc - kw // 2, 0, w - kw)
    inwin = (kc >= cs) & (kc < cs + kw)
    sel_c = (inwin[:, :, None]
             & (np.arange(2 * kw - 1)[None, None, :] == (kc - qc + kw - 1)[:, :, None])).astype(np.float32)
    var = np.arange(kh)[:, None]
    ii = np.arange(kh)[None, :]
    sel_r = (np.arange(2 * kh - 1)[None, None, :] == (ii - var + kh - 1)[:, :, None]).astype(np.float32)
    t = jnp.einsum('lhrc,vir,qkc->lvhqik', rpb * LOG2E, sel_r, sel_c,
                   precision=lax.Precision.HIGHEST)
    t = t + np.where(inwin, 0.0, NEG_BIAS).astype(np.float32)[:, None, :]
    return t.reshape(n_l, kh, h // 2, 2 * w, kh * w)


def _na_kernel(q_ref, k_ref, v_ref, bias_ref, o_ref, *, rows, n_pairs):
    w = GRID_W
    band = NA_KH * w
    lane = lax.broadcasted_iota(jnp.int32, (w, LANES), 1)
    low = lane < HEAD_DIM

    def body(it, carry):
        chains = []
        for rr in range(NA_ROWS_PER_STEP):
            r = it * NA_ROWS_PER_STEP + rr
            rs = jnp.clip(r - NA_KH // 2, 0, rows - NA_KH)
            q0 = pl.multiple_of(r * w, w)
            k0 = pl.multiple_of(rs * w, w)
            for pr in range(n_pairs):
                chains.append((r - rs, q0, k0, pr, slice(pr * LANES, (pr + 1) * LANES)))
        scores = []
        for var, q0, k0, pr, sl in chains:
            q2 = q_ref[0, pl.ds(q0, w), sl]
            zero = jnp.zeros_like(q2)
            qq = jnp.concatenate([jnp.where(low, q2, zero), jnp.where(low, zero, q2)], axis=0)
            kk = k_ref[0, pl.ds(k0, band), sl]
            s = lax.dot_general(qq, kk, (((1,), (1,)), ((), ())), preferred_element_type=F32)
            scores.append(s + bias_ref[var, pr])
        probs = []
        for s in scores:
            e = jnp.exp2(s - jnp.max(s, axis=-1, keepdims=True))
            probs.append((e.astype(BF16), jnp.sum(e, axis=-1, keepdims=True)))
        for (var, q0, k0, pr, sl), (e, l) in zip(chains, probs):
            vv = v_ref[0, pl.ds(k0, band), sl]
            pv = jnp.dot(e, vv, preferred_element_type=F32) / l
            o = jnp.where(low, pv[0:w], pv[w:2 * w])
            o_ref[0, pl.ds(q0, w), sl] = o.astype(o_ref.dtype)
        return carry

    lax.fori_loop(0, rows // NA_ROWS_PER_STEP, body, 0)


def _neighbourhood_attention(q, k, v, bias, l):
    b, s, c = q.shape
    rows = s // GRID_W
    assert rows >= NA_KH and rows % NA_ROWS_PER_STEP == 0 and c % LANES == 0
    n_pairs = c // LANES
    seq = pl.BlockSpec((1, s, c), lambda i: (i, 0, 0))
    return pl.pallas_call(
        functools.partial(_na_kernel, rows=rows, n_pairs=n_pairs),
        grid=(b,),
        in_specs=[seq, seq, seq, _layer_spec(bias, l)],
        out_specs=seq,
        out_shape=jax.ShapeDtypeStruct((b, s, c), BF16),
        compiler_params=_params("parallel"),
        name="neighbourhood_attn",
    )(q, k, v, bias)


def _tail_kernel(yf_ref, s_ref, p_ref, hn_ref, pg_ref, yn_ref, x_ref, mod_ref, modp_ref, wo_ref,
                 g_ref, wg_ref, wu_ref, wd_ref, *rest, d_ff, splits):
    if splits is None:
        gf_ref, o_ref = rest[:2]
    else:
        modn_ref, gn_ref, win_ref, o_ref = rest[:4]
        p_refs = rest[4:-3]
    act_ref, xmid_ref, hffn_ref = rest[-3:]
    tm = x_ref.shape[1]
    n_lt = yf_ref.shape[1]
    ts = LRU_TILE

    @pl.when(pl.program_id(0) == 0)
    def _():
        xmid_ref[...] = jnp.zeros_like(xmid_ref)
        hffn_ref[...] = jnp.zeros_like(hffn_ref)

    h_bwd_tail = jnp.concatenate(
        [p_ref[0, k * ts:(k + 1) * ts, :].astype(F32) * hn_ref[0, 0, k:k + 1, :]
         for k in range(tm // ts)], axis=0)
    yl = (_gelu_tanh(pg_ref[0].astype(F32)) * (s_ref[0].astype(F32) + h_bwd_tail)).astype(BF16)
    yf = [jnp.concatenate([yf_ref[0, lt, a * ROW_PITCH:a * ROW_PITCH + GRID_W, :]
                           for a in range(tm // GRID_W)], axis=0).astype(BF16) for lt in range(n_lt)]
    mixed = jnp.concatenate(yf + [yl, yn_ref[0]], axis=1)

    h = hffn_ref[...]
    for j in range(d_ff // FF_CHUNK):
        cs = slice(j * FF_CHUNK, (j + 1) * FF_CHUNK)
        gt = jnp.dot(h, wg_ref[:, cs], preferred_element_type=F32)
        up = jnp.dot(h, wu_ref[:, cs], preferred_element_type=F32)
        act_ref[:, cs] = (gt * jax.nn.sigmoid(gt) * up).astype(BF16)
    f = jnp.dot(act_ref[...], wd_ref[...], preferred_element_type=F32)

    y = jnp.dot(mixed, wo_ref[...], preferred_element_type=F32)
    x = xmid_ref[...] + modp_ref[0, 5:6, :] * f
    if splits is None:
        ms = jnp.mean(x * x, axis=-1, keepdims=True)
        o_ref[0] = x * lax.rsqrt(ms + EPS) * gf_ref[...]
    else:
        o_ref[0] = x
        hn = _norm_mod(x, gn_ref[...], modn_ref[0, 0:1, :], modn_ref[0, 1:2, :]).astype(BF16)

    xm = x_ref[0] + mod_ref[0, 2:3, :] * y
    xmid_ref[...] = xm
    hffn_ref[...] = _norm_mod(xm, g_ref[...], mod_ref[0, 3:4, :], mod_ref[0, 4:5, :]).astype(BF16)

    if splits is not None:
        _emit_in_proj(hn, win_ref, p_refs, splits)


def _layer_tail(y_f, s_lru, p_lru, h_next, p_g, y_n, x, mod, w_out, g_ffn, w_gate, w_up, w_down, l,
                nxt=None, g_final=None, splits=None):
    b, s, d = x.shape
    tm = TOKEN_TILE
    d_ff = w_gate.shape[-1]
    assert d_ff % FF_CHUNK == 0 and tm % GRID_W == 0 and tm % LRU_TILE == 0
    nt = s // tm
    n_tiles = b * nt
    cur = lambda k: jnp.minimum(k, n_tiles - 1)
    prv = lambda k: jnp.maximum(k - 1, 0)
    tok_in = lambda w: pl.BlockSpec((1, tm, w), lambda k: (cur(k) // nt, cur(k) % nt, 0))
    mod_at = lambda layer, sel: pl.BlockSpec((None, 1) + mod.shape[2:],
                                             lambda k: (layer, sel(k) // nt, 0, 0))
    in_specs = [pl.BlockSpec((1, y_f.shape[1], tm // GRID_W * ROW_PITCH, LANES),
                             lambda k: (cur(k) // nt, 0, cur(k) % nt, 0)),
                tok_in(s_lru.shape[-1]), tok_in(p_lru.shape[-1]),
                pl.BlockSpec((1, 1) + h_next.shape[2:], lambda k: (cur(k) // nt, cur(k) % nt, 0, 0)),
                tok_in(p_g.shape[-1]), tok_in(y_n.shape[-1]), tok_in(d),
                mod_at(l, cur), mod_at(l, prv), _layer_spec(w_out, l), _layer_spec(g_ffn, l),
                _layer_spec(w_gate, l), _layer_spec(w_up, l), _layer_spec(w_down, l)]
    args = [y_f, s_lru, p_lru, h_next, p_g, y_n, x, mod, mod, w_out, g_ffn, w_gate, w_up, w_down]
    out_shape = [jax.ShapeDtypeStruct((b, s, d), F32)]
    out_specs = [pl.BlockSpec((1, tm, d), lambda k: (prv(k) // nt, prv(k) % nt, 0))]
    if nxt is None:
        in_specs += [_const_spec(g_final.shape)]
        args += [g_final]
        splits = None
    else:
        g_mix, w_in = nxt
        in_specs += [mod_at(l + 1, prv), _layer_spec(g_mix, l + 1), _layer_spec(w_in, l + 1)]
        args += [mod, g_mix, w_in]
        shapes, specs = _in_proj_out(b, s, tm, splits, lambda k: (prv(k) // nt, prv(k) % nt))
        out_shape += shapes
        out_specs += specs
    return pl.pallas_call(
        functools.partial(_tail_kernel, d_ff=d_ff, splits=splits),
        grid=(n_tiles + 1,),
        in_specs=in_specs,
        out_specs=out_specs,
        out_shape=out_shape,
        scratch_shapes=[pltpu.VMEM((tm, d_ff), BF16),
                        pltpu.VMEM((tm, d), F32),
                        pltpu.VMEM((tm, d), BF16)],
        compiler_params=pltpu.CompilerParams(dimension_semantics=("arbitrary",),
                                             vmem_limit_bytes=TAIL_VMEM_LIMIT),
        name="layer_tail",
    )(*args)


def kernel(x, c, w_ada, b_ada, g_mix, g_ffn, w_in, w_fourier, conv_w, conv_b, lru_w_a, lru_b_a,
           lru_w_x, lru_b_x, lru_lambda, na_rpb, w_out, w_ffn_gate, w_ffn_up, w_ffn_down, g_final):
    b, s, d = x.shape
    depth = w_in.shape[0]
    n_groups = w_fourier.shape[1]
    d_f = n_groups * HEAD_DIM
    d_l = conv_w.shape[-1]
    d_n = na_rpb.shape[1] * HEAD_DIM
    tm = TOKEN_TILE

    mod = _modulation(c, w_ada, b_ada).reshape(depth, b, N_MOD, d)
    tables = _fourier_tables(n_groups)
    splits = _in_proj_splits(d_f, d_l, d_n)
    w_in_b = w_in.astype(BF16)
    w_out_b = w_out.astype(BF16)
    w_gate_b = w_ffn_gate.astype(BF16)
    w_up_b = w_ffn_up.astype(BF16)
    w_down_b = w_ffn_down.astype(BF16)
    g_mix3 = g_mix.reshape(depth, 1, d)
    g_ffn3 = g_ffn.reshape(depth, 1, d)
    w_fourier_b = _block_diag(w_fourier).astype(BF16)
    w_gates = _gate_weights(lru_w_a, lru_b_a, lru_w_x, lru_b_x)
    conv_b3 = conv_b.reshape(depth, 1, d_l)
    na_bias = _na_bias(na_rpb)

    p_f, p_x, p_g, q, k, v = _in_proj(x, mod, g_mix3, w_in_b, 0, splits)
    for l in range(depth):
        y_f = _fourier_mix(p_f, w_fourier_b, l, tables)
        s_lru, p_lru, h_next = _rg_lru(p_x, conv_w, conv_b3, w_gates, lru_lambda, l)
        h_next = jnp.transpose(h_next, (1, 0, 2)).reshape(b, s // tm, tm // LRU_TILE, d_l)
        y_n = _neighbourhood_attention(q, k, v, na_bias, l)
        common = (y_f, s_lru, p_lru, h_next, p_g, y_n, x, mod, w_out_b, g_ffn3,
                  w_gate_b, w_up_b, w_down_b, l)
        if l + 1 < depth:
            x, p_f, p_x, p_g, q, k, v = _layer_tail(*common, nxt=(g_mix3, w_in_b), splits=splits)
        else:
            (x,) = _layer_tail(*common, g_final=g_final.reshape(1, d))
    return x
```

```python
import functools

import numpy as np
import jax
import jax.numpy as jnp
from jax import lax
from jax.experimental import pallas as pl
from jax.experimental.pallas import tpu as pltpu

F32 = jnp.float32
BF16 = jnp.bfloat16

HEAD_DIM = 64
GRID_W = 64
CONV_W = 4
LRU_C = 8.0
NA_KH = 8
NA_KW = 16
N_MOD = 6
EPS = 1e-6
NEG_BIAS = -1e30
LOG2E = 1.4426950408889634
NA_ROWS_PER_STEP = 2

LANES = 128
SUBLANES = 8
VMEM_LIMIT = 56 * 1024 * 1024
TAIL_VMEM_LIMIT = 60 * 1024 * 1024

TOKEN_TILE = 512
FF_CHUNK = 256
LRU_TILE = 256
LRU_GROUP = 8


def _params(*sem):
    return pltpu.CompilerParams(dimension_semantics=sem, vmem_limit_bytes=VMEM_LIMIT)


def _const_spec(shape):
    nd = len(shape)
    return pl.BlockSpec(shape, lambda *_: (0,) * nd, pipeline_mode=pl.Buffered(1))


def _layer_spec(arr, l):
    nd = arr.ndim - 1
    return pl.BlockSpec((None,) + arr.shape[1:], lambda *_: (l,) + (0,) * nd,
                        pipeline_mode=pl.Buffered(1))


def _mod_spec(mod, l):
    return pl.BlockSpec((None, 1) + mod.shape[2:], lambda i, j: (l, i, 0, 0))


def _norm_mod(x, g, shift, scale):
    ms = jnp.mean(x * x, axis=-1, keepdims=True)
    return (x * lax.rsqrt(ms + EPS)) * (g * (1.0 + scale)) + shift


def _gelu_tanh(x):
    return 0.5 * x * (1.0 + jnp.tanh(0.7978845608028654 * (x + 0.044715 * (x * x * x))))


def _mod_kernel(c_ref, w_ref, b_ref, o_ref):
    c = c_ref[...]
    ca = c * jax.nn.sigmoid(c)
    o_ref[0] = jnp.dot(ca, w_ref[0], precision=lax.Precision.HIGHEST,
                       preferred_element_type=F32) + b_ref[0]


def _modulation(c, w_ada, b_ada):
    depth, d, n = w_ada.shape
    b = c.shape[0]
    tn = 1024
    return pl.pallas_call(
        _mod_kernel,
        grid=(depth, n // tn),
        in_specs=[pl.BlockSpec((b, d), lambda l, j: (0, 0)),
                  pl.BlockSpec((1, d, tn), lambda l, j: (l, 0, j)),
                  pl.BlockSpec((1, 1, tn), lambda l, j: (l, 0, j))],
        out_specs=pl.BlockSpec((1, b, tn), lambda l, j: (l, 0, j)),
        out_shape=jax.ShapeDtypeStruct((depth, b, n), F32),
        compiler_params=_params("arbitrary", "arbitrary"),
        name="adaln_mod",
    )(c, w_ada, b_ada.reshape(depth, 1, n))


ROW_PITCH = GRID_W + SUBLANES


def _in_proj_splits(d_f, d_l, d_n):
    o1 = d_f
    o2 = o1 + d_l
    o3 = o2 + d_l
    o4 = o3 + d_n
    o5 = o4 + d_n
    o6 = o5 + d_n
    return ((0, o1, 1.0), (o1, o2, 1.0), (o2, o3, 1.0),
            (o3, o4, HEAD_DIM ** -0.5 * LOG2E), (o4, o5, 1.0), (o5, o6, 1.0))


def _in_proj_out(b, s, tm, splits, tile_of=lambda i, j: (i, j)):
    (lo, hi, _), rest = splits[0], splits[1:]
    rows = s // GRID_W

    def pf_map(*g):
        i, j = tile_of(*g)
        return i, 0, j, 0

    def tok_map(*g):
        i, j = tile_of(*g)
        return i, j, 0

    shapes = [jax.ShapeDtypeStruct((b, (hi - lo) // LANES, rows * ROW_PITCH, LANES), F32)]
    specs = [pl.BlockSpec((1, (hi - lo) // LANES, tm // GRID_W * ROW_PITCH, LANES), pf_map)]
    for lo, hi, _ in rest:
        shapes.append(jax.ShapeDtypeStruct((b, s, hi - lo), BF16))
        specs.append(pl.BlockSpec((1, tm, hi - lo), tok_map))
    return shapes, specs


IN_PROJ_GROUPS = ((1, 2), (3, 4), (5,))


def _emit_in_proj(h, w_ref, out_refs, splits, row0=0):
    pf_ref = out_refs[0]
    lo, hi, _ = splits[0]
    n_rows = h.shape[0]
    p = jnp.dot(h, w_ref[:, lo:hi], preferred_element_type=F32)
    pad = jnp.zeros((ROW_PITCH - GRID_W, LANES), F32)
    for lt in range((hi - lo) // LANES):
        for a in range(n_rows // GRID_W):
            dst = (row0 // GRID_W + a) * ROW_PITCH
            pf_ref[0, lt, dst:dst + GRID_W, :] = p[a * GRID_W:(a + 1) * GRID_W, lt * LANES:(lt + 1) * LANES]
            pf_ref[0, lt, dst + GRID_W:dst + ROW_PITCH, :] = pad
    for group in IN_PROJ_GROUPS:
        g_lo, g_hi = splits[group[0]][0], splits[group[-1]][1]
        p = jnp.dot(h, w_ref[:, g_lo:g_hi], preferred_element_type=F32)
        for idx in group:
            lo, hi, scale = splits[idx]
            part = p[:, lo - g_lo:hi - g_lo]
            if scale != 1.0:
                part = part * scale
            out_refs[idx][0, row0:row0 + n_rows, :] = part.astype(out_refs[idx].dtype)


def _in_proj_kernel(x_ref, mod_ref, g_ref, w_ref, *out_refs, splits):
    h = _norm_mod(x_ref[0], g_ref[...], mod_ref[0, 0:1, :], mod_ref[0, 1:2, :]).astype(BF16)
    _emit_in_proj(h, w_ref, out_refs, splits)


def _in_proj(x, mod, g_mix, w_in, l, splits):
    b, s, d = x.shape
    tm = TOKEN_TILE
    shapes, specs = _in_proj_out(b, s, tm, splits)
    return pl.pallas_call(
        functools.partial(_in_proj_kernel, splits=splits),
        grid=(b, s // tm),
        in_specs=[pl.BlockSpec((1, tm, d), lambda i, j: (i, j, 0)),
                  _mod_spec(mod, l), _layer_spec(g_mix, l), _layer_spec(w_in, l)],
        out_specs=specs,
        out_shape=shapes,
        compiler_params=_params("parallel", "arbitrary"),
        name="in_proj",
    )(x, mod, g_mix, w_in)


FOURIER_SLAB = 2 * GRID_W + SUBLANES
FOURIER_COLS = 8


def _fourier_tables(n_groups):
    n = GRID_W
    idx = np.arange(n)
    ang1 = 2.0 * np.pi * np.outer(idx, idx) / n
    f1 = np.concatenate([np.cos(ang1), np.sin(ang1)], axis=0)
    c = idx[:, None, None]
    dd = idx[None, :, None]
    bb = idx[None, None, :]
    ang2 = 2.0 * np.pi * bb * (n * dd + c) / (n * n)
    gc, gs = np.cos(ang2), np.sin(ang2)
    g2 = np.concatenate([np.concatenate([gc, -gs], axis=2),
                         np.concatenate([gs, gc], axis=2)], axis=1)
    angd = 2.0 * np.pi * np.outer(np.arange(HEAD_DIM), np.arange(HEAD_DIM)) / HEAD_DIM
    eye = np.eye(n_groups)
    cs = np.concatenate([np.kron(eye, np.cos(angd)), -np.kron(eye, np.sin(angd))], axis=0)
    return jnp.asarray(f1, BF16), jnp.asarray(g2, BF16), jnp.asarray(cs, BF16)


def _fourier_kernel(x_ref, f1_ref, g2_ref, cs_ref, w_ref, o_ref, z_ref, y_ref, *, width, scale):
    n = GRID_W
    n_lt = width // LANES
    f1 = f1_ref[...]
    for b in range(n):
        xb = jnp.concatenate([x_ref[0, lt, pl.ds(b, n, stride=ROW_PITCH), :] for lt in range(n_lt)],
                             axis=1).astype(BF16)
        slab = jnp.dot(f1, xb, preferred_element_type=F32)
        for lt in range(n_lt):
            z_ref[lt, b * FOURIER_SLAB:b * FOURIER_SLAB + 2 * n, :] = slab[:, lt * LANES:(lt + 1) * LANES]
    for c in range(n):
        z = jnp.concatenate(
            [jnp.concatenate([z_ref[lt, pl.ds(off + c, n, stride=FOURIER_SLAB), :]
                              for lt in range(n_lt)], axis=1)
             for off in (0, n)], axis=0).astype(BF16)
        y = jnp.dot(g2_ref[c], z, preferred_element_type=F32)
        y_ref[c * n:(c + 1) * n, 0:width] = y[0:n].astype(BF16)
        y_ref[c * n:(c + 1) * n, width:2 * width] = y[n:2 * n].astype(BF16)
    cs = cs_ref[...]
    w = w_ref[...]
    for blk0 in range(0, n // FOURIER_COLS, 2):
        blks = (blk0, blk0 + 1)
        fs = [jnp.dot(y_ref[blk * FOURIER_COLS * n:(blk + 1) * FOURIER_COLS * n, :], cs,
                      preferred_element_type=F32) * scale for blk in blks]
        outs = [jnp.dot(f.astype(BF16), w, preferred_element_type=F32) for f in fs]
        for blk, out in zip(blks, outs):
            for cc in range(FOURIER_COLS):
                c = blk * FOURIER_COLS + cc
                for lt in range(n_lt):
                    o_ref[0, lt, pl.ds(c, n, stride=ROW_PITCH), :] = \
                        out[cc * n:(cc + 1) * n, lt * LANES:(lt + 1) * LANES]
    pad = jnp.zeros((ROW_PITCH - n, LANES), F32)
    for lt in range(n_lt):
        for dd in range(n):
            o_ref[0, lt, dd * ROW_PITCH + n:(dd + 1) * ROW_PITCH, :] = pad


def _fourier_mix(p_f, w_blk, l, tables):
    b, n_lt, prow, _ = p_f.shape
    n = GRID_W
    assert prow == n * ROW_PITCH, "sequence DFT is factored as 64 x 64"
    width = n_lt * LANES
    f1, g2, cs = tables
    scale = float(1.0 / np.sqrt(n * n * HEAD_DIM))
    blk = pl.BlockSpec((1, n_lt, prow, LANES), lambda i: (i, 0, 0, 0))
    return pl.pallas_call(
        functools.partial(_fourier_kernel, width=width, scale=scale),
        grid=(b,),
        in_specs=[blk,
                  _const_spec(f1.shape), _const_spec(g2.shape),
                  _const_spec(cs.shape), _layer_spec(w_blk, l)],
        out_specs=blk,
        out_shape=jax.ShapeDtypeStruct(p_f.shape, F32),
        scratch_shapes=[pltpu.VMEM((n_lt, n * FOURIER_SLAB, LANES), F32),
                        pltpu.VMEM((n * n, 2 * width), BF16)],
        compiler_params=_params("parallel"),
        name="fourier_mix",
    )(p_f, f1, g2, cs, w_blk)


LRU_HALO = 2 * SUBLANES
LRU_XSTRIDE = LRU_TILE + 2 * LRU_HALO + SUBLANES
LRU_CONV_LEFT = 2
LRU_CHUNK = 16
LRU_BIAS_ROWS = 2


def _softplus_neg(lam):
    e = jnp.exp(-jnp.abs(lam))
    u = 1.0 + e
    l1p = jnp.where(u == 1.0, e, jnp.log(u) * (e / (u - 1.0)))
    return jnp.maximum(-lam, 0.0) + l1p


def _lru_kernel(x_ref, xp_ref, xn_ref, cw_ref, cb_ref, w_ref, lam_ref,
                s_ref, p_ref, carry_ref,
                xs_ref, xt_ref, af_ref, bf_ref, ab_ref, bb_ref, hs_ref, ps_ref,
                cf_ref, lsum_ref, psum_ref, *, nt):
    ts, grp = LRU_TILE, LRU_GROUP
    n_lt = xs_ref.shape[0]
    c = n_lt * LANES
    lanes = lambda lt: slice(lt * LANES, (lt + 1) * LANES)
    i = pl.program_id(1)
    first = i == 0
    last = i == nt - 1

    @pl.when(first)
    def _():
        cf_ref[...] = jnp.zeros_like(cf_ref)

    for j in range(grp):
        base = j * LRU_XSTRIDE
        for lt in range(n_lt):
            xs_ref[lt, base:base + LRU_HALO, :] = \
                jnp.where(first, 0.0, xp_ref[j, :, lanes(lt)].astype(F32))
            xs_ref[lt, base + LRU_HALO:base + LRU_HALO + ts, :] = x_ref[j, :, lanes(lt)].astype(F32)
            xs_ref[lt, base + LRU_HALO + ts:base + 2 * LRU_HALO + ts, :] = \
                jnp.where(last, 0.0, xn_ref[j, :, lanes(lt)].astype(F32))

    def to_time_major(tt, carry):
        r0 = pl.multiple_of(tt * grp, grp)
        src = pl.ds(LRU_HALO - LRU_CONV_LEFT + tt, grp, stride=LRU_XSTRIDE)
        for lt in range(n_lt):
            xt_ref[pl.ds(r0, grp), lanes(lt)] = xs_ref[lt, src, :]
        return carry

    lax.fori_loop(0, ts + CONV_W - 1, to_time_major, 0, unroll=True)

    cw = cw_ref[...]
    cb = cb_ref[...]
    coef = [(0.5 * LRU_C) * _softplus_neg(lam_ref[d:d + 1, :]) for d in range(2)]
    rows_c = LRU_CHUNK * grp
    bias_cols = jnp.where(lax.broadcasted_iota(jnp.int32, (rows_c, LANES), 1) < LRU_BIAS_ROWS,
                          1.0, 0.0).astype(BF16)

    def gate_chunk(ch, carry):
        r0 = pl.multiple_of(ch * rows_c, rows_c)
        u = cb
        for k in range(CONV_W):
            u = u + cw[k:k + 1, :] * xt_ref[pl.ds(r0 + k * grp, rows_c), :]
        ub = jnp.concatenate([u.astype(BF16), bias_cols], axis=1)
        uh = 0.5 * u
        for d, (a_ref, b_ref) in enumerate(((af_ref, bf_ref), (ab_ref, bb_ref))):
            z = jnp.dot(ub, w_ref[d], preferred_element_type=F32)
            tha = jnp.tanh(z[:, 0:c])
            thx = jnp.tanh(z[:, c:2 * c])
            nla = coef[d] * (tha + 1.0)
            a = jnp.exp2(nla * (-LOG2E))
            m = jnp.tanh(nla) * (1.0 + a * a)
            sq = jnp.where(m > 0.0, m * lax.rsqrt(m), 0.0)
            a_ref[pl.ds(r0, rows_c), :] = a
            b_ref[pl.ds(r0, rows_c), :] = sq * ((thx + 1.0) * uh)
        return carry

    lax.fori_loop(0, ts // LRU_CHUNK, gate_chunk, 0, unroll=True)

    def bwd_step(s, carry):
        h, p = carry
        rows = pl.ds(pl.multiple_of((ts - 1 - s) * grp, grp), grp)
        a = [ab_ref[rows, lanes(lt)] for lt in range(n_lt)]
        h = tuple(a[lt] * h[lt] + bb_ref[rows, lanes(lt)] for lt in range(n_lt))
        p = tuple(a[lt] * p[lt] for lt in range(n_lt))
        for lt in range(n_lt):
            hs_ref[lt, rows, :] = h[lt]
            ps_ref[lt, rows, :] = p[lt]
        return h, p

    zero = jnp.zeros((grp, LANES), F32)
    h, p = lax.fori_loop(0, ts, bwd_step, ((zero,) * n_lt, (zero + 1.0,) * n_lt), unroll=True)
    for lt in range(n_lt):
        lsum_ref[i, :, lanes(lt)] = h[lt]
        psum_ref[i, :, lanes(lt)] = p[lt]

    def fwd_step(t, h):
        rows = pl.ds(pl.multiple_of(t * grp, grp), grp)
        h = tuple(af_ref[rows, lanes(lt)] * h[lt] + bf_ref[rows, lanes(lt)] for lt in range(n_lt))
        for lt in range(n_lt):
            hs_ref[lt, rows, :] = hs_ref[lt, rows, :] + h[lt]
        return h

    h = lax.fori_loop(0, ts, fwd_step, tuple(cf_ref[:, lanes(lt)] for lt in range(n_lt)), unroll=True)
    for lt in range(n_lt):
        cf_ref[:, lanes(lt)] = h[lt]

    pack = 2 * SUBLANES
    for src_ref, o_ref in ((hs_ref, s_ref), (ps_ref, p_ref)):
        for j in range(grp):
            for tb in range(ts // pack):
                for lt in range(n_lt):
                    o_ref[j, tb * pack:(tb + 1) * pack, lanes(lt)] = jnp.concatenate(
                        [src_ref[lt, pl.ds((tb * pack + k * SUBLANES) * grp + j, SUBLANES, stride=grp), :]
                         for k in range(2)], axis=0).astype(o_ref.dtype)

    @pl.when(last)
    def _():
        hn = jnp.zeros((grp, c), F32)
        for t in reversed(range(nt)):
            carry_ref[t] = hn
            hn = lsum_ref[t] + psum_ref[t] * hn


def _gate_weights(w_a, b_a, w_x, b_x):
    w = jnp.concatenate([_block_diag(0.5 * w_a), _block_diag(0.5 * w_x)], axis=-1)
    b = 0.5 * jnp.concatenate([b_a, b_x], axis=-1)
    b_hi = b.astype(BF16).astype(F32)
    extra = jnp.stack([b_hi, b - b_hi], axis=-2)
    extra = jnp.pad(extra, [(0, 0), (0, 0), (0, LANES - LRU_BIAS_ROWS), (0, 0)])
    return jnp.concatenate([w, extra], axis=-2).astype(BF16)


def _rg_lru(p_x, conv_w, conv_b, w_gates, lam, l):
    b, s, c = p_x.shape
    ts, grp, halo = LRU_TILE, LRU_GROUP, LRU_HALO
    assert b % grp == 0 and s % ts == 0
    assert grp == SUBLANES and ts % LRU_CHUNK == 0 and c % LANES == 0
    nt = s // ts
    hb_per_tile = ts // halo
    n_halo = s // halo
    n_lt = c // LANES
    cur = pl.BlockSpec((grp, ts, c), lambda g, i: (g, i, 0))
    prev = pl.BlockSpec((grp, halo, c), lambda g, i: (g, jnp.maximum(i * hb_per_tile - 1, 0), 0))
    nxt = pl.BlockSpec((grp, halo, c),
                       lambda g, i: (g, jnp.minimum((i + 1) * hb_per_tile, n_halo - 1), 0))
    tile_f32 = pltpu.VMEM((ts * grp, c), F32)
    planes_f32 = pltpu.VMEM((n_lt, ts * grp, LANES), F32)
    return pl.pallas_call(
        functools.partial(_lru_kernel, nt=nt),
        grid=(b // grp, nt),
        in_specs=[cur, prev, nxt,
                  _layer_spec(conv_w, l), _layer_spec(conv_b, l), _layer_spec(w_gates, l),
                  _layer_spec(lam, l)],
        out_specs=[cur, cur, pl.BlockSpec((nt, grp, c), lambda g, i: (0, g, 0))],
        out_shape=[jax.ShapeDtypeStruct((b, s, c), BF16), jax.ShapeDtypeStruct((b, s, c), BF16),
                   jax.ShapeDtypeStruct((nt, b, c), F32)],
        scratch_shapes=[pltpu.VMEM((n_lt, grp * LRU_XSTRIDE, LANES), F32),
                        pltpu.VMEM(((ts + CONV_W - 1) * grp, c), F32),
                        tile_f32, tile_f32, tile_f32, tile_f32,
                        planes_f32, planes_f32,
                        pltpu.VMEM((grp, c), F32),
                        pltpu.VMEM((nt, grp, c), F32),
                        pltpu.VMEM((nt, grp, c), F32)],
        compiler_params=_params("parallel", "arbitrary"),
        name="rg_lru",
    )(p_x, p_x, p_x, conv_w, conv_b, w_gates, lam)


def _block_diag(w):
    h, dh = w.shape[-3], w.shape[-1]
    lead = [(0, 0)] * (w.ndim - 2)
    rows = [jnp.pad(w[..., k, :, :], lead + [(k * dh, (h - 1 - k) * dh)]) for k in range(h)]
    return jnp.concatenate(rows, axis=-2)


def _na_bias(rpb):
    n_l, h = rpb.shape[:2]
    w, kh, kw = GRID_W, NA_KH, NA_KW
    qc = np.arange(w)[:, None]
    kc = np.arange(w)[None, :]
    cs = np.clip(qc - kw // 2, 0, w - kw)
    inwin = (kc >= cs) & (kc < cs + kw)
    sel_c = (inwin[:, :, None]
             & (np.arange(2 * kw - 1)[None, None, :] == (kc - qc + kw - 1)[:, :, None])).astype(np.float32)
    var = np.arange(kh)[:, None]
    ii = np.arange(kh)[None, :]
    sel_r = (np.arange(2 * kh - 1)[None, None, :] == (ii - var + kh - 1)[:, :, None]).astype(np.float32)
    t = jnp.einsum('lhrc,vir,qkc->lvhqik', rpb * LOG2E, sel_r, sel_c,
                   precision=lax.Precision.HIGHEST)
    t = t + np.where(inwin, 0.0, NEG_BIAS).astype(np.float32)[:, None, :]
    return t.reshape(n_l, kh, h // 2, 2 * w, kh * w)


def _na_kernel(q_ref, k_ref, v_ref, bias_ref, o_ref, *, rows, n_pairs):
    w = GRID_W
    band = NA_KH * w
    lane = lax.broadcasted_iota(jnp.int32, (w, LANES), 1)
    low = lane < HEAD_DIM

    def body(it, carry):
        chains = []
        for rr in range(NA_ROWS_PER_STEP):
            r = it * NA_ROWS_PER_STEP + rr
            rs = jnp.clip(r - NA_KH // 2, 0, rows - NA_KH)
            q0 = pl.multiple_of(r * w, w)
            k0 = pl.multiple_of(rs * w, w)
            for pr in range(n_pairs):
                chains.append((r - rs, q0, k0, pr, slice(pr * LANES, (pr + 1) * LANES)))
        scores = []
        for var, q0, k0, pr, sl in chains:
            q2 = q_ref[0, pl.ds(q0, w), sl]
            zero = jnp.zeros_like(q2)
            qq = jnp.concatenate([jnp.where(low, q2, zero), jnp.where(low, zero, q2)], axis=0)
            kk = k_ref[0, pl.ds(k0, band), sl]
            s = lax.dot_general(qq, kk, (((1,), (1,)), ((), ())), preferred_element_type=F32)
            scores.append(s + bias_ref[var, pr])
        probs = []
        for s in scores:
            e = jnp.exp2(s - jnp.max(s, axis=-1, keepdims=True))
            probs.append((e.astype(BF16), jnp.sum(e, axis=-1, keepdims=True)))
        for (var, q0, k0, pr, sl), (e, l) in zip(chains, probs):
            vv = v_ref[0, pl.ds(k0, band), sl]
            pv = jnp.dot(e, vv, preferred_element_type=F32) / l
            o = jnp.where(low, pv[0:w], pv[w:2 * w])
            o_ref[0, pl.ds(q0, w), sl] = o.astype(o_ref.dtype)
        return carry

    lax.fori_loop(0, rows // NA_ROWS_PER_STEP, body, 0)


def _neighbourhood_attention(q, k, v, bias, l):
    b, s, c = q.shape
    rows = s // GRID_W
    assert rows >= NA_KH and rows % NA_ROWS_PER_STEP == 0 and c % LANES == 0
    n_pairs = c // LANES
    seq = pl.BlockSpec((1, s, c), lambda i: (i, 0, 0))
    return pl.pallas_call(
        functools.partial(_na_kernel, rows=rows, n_pairs=n_pairs),
        grid=(b,),
        in_specs=[seq, seq, seq, _layer_spec(bias, l)],
        out_specs=seq,
        out_shape=jax.ShapeDtypeStruct((b, s, c), BF16),
        compiler_params=_params("parallel"),
        name="neighbourhood_attn",
    )(q, k, v, bias)


def _tail_kernel(yf_ref, s_ref, p_ref, hn_ref, pg_ref, yn_ref, x_ref, mod_ref, modp_ref, wo_ref,
                 g_ref, wg_ref, wu_ref, wd_ref, *rest, d_ff, splits):
    if splits is None:
        gf_ref, o_ref = rest[:2]
    else:
        modn_ref, gn_ref, win_ref, o_ref = rest[:4]
        p_refs = rest[4:-3]
    act_ref, xmid_ref, hffn_ref = rest[-3:]
    tm = x_ref.shape[1]
    n_lt = yf_ref.shape[1]
    ts = LRU_TILE

    @pl.when(pl.program_id(0) == 0)
    def _():
        xmid_ref[...] = jnp.zeros_like(xmid_ref)
        hffn_ref[...] = jnp.zeros_like(hffn_ref)

    h_bwd_tail = jnp.concatenate(
        [p_ref[0, k * ts:(k + 1) * ts, :].astype(F32) * hn_ref[0, 0, k:k + 1, :]
         for k in range(tm // ts)], axis=0)
    yl = (_gelu_tanh(pg_ref[0].astype(F32)) * (s_ref[0].astype(F32) + h_bwd_tail)).astype(BF16)
    yf = [jnp.concatenate([yf_ref[0, lt, a * ROW_PITCH:a * ROW_PITCH + GRID_W, :]
                           for a in range(tm // GRID_W)], axis=0).astype(BF16) for lt in range(n_lt)]
    mixed = jnp.concatenate(yf + [yl, yn_ref[0]], axis=1)

    h = hffn_ref[...]
    for j in range(d_ff // FF_CHUNK):
        cs = slice(j * FF_CHUNK, (j + 1) * FF_CHUNK)
        gt = jnp.dot(h, wg_ref[:, cs], preferred_element_type=F32)
        up = jnp.dot(h, wu_ref[:, cs], preferred_element_type=F32)
        act_ref[:, cs] = (gt * jax.nn.sigmoid(gt) * up).astype(BF16)
    f = jnp.dot(act_ref[...], wd_ref[...], preferred_element_type=F32)

    y = jnp.dot(mixed, wo_ref[...], preferred_element_type=F32)
    x = xmid_ref[...] + modp_ref[0, 5:6, :] * f
    if splits is None:
        ms = jnp.mean(x * x, axis=-1, keepdims=True)
        o_ref[0] = x * lax.rsqrt(ms + EPS) * gf_ref[...]
    else:
        o_ref[0] = x
        hn = _norm_mod(x, gn_ref[...], modn_ref[0, 0:1, :], modn_ref[0, 1:2, :]).astype(BF16)

    xm = x_ref[0] + mod_ref[0, 2:3, :] * y
    xmid_ref[...] = xm
    hffn_ref[...] = _norm_mod(xm, g_ref[...], mod_ref[0, 3:4, :], mod_ref[0, 4:5, :]).astype(BF16)

    if splits is not None:
        _emit_in_proj(hn, win_ref, p_refs, splits)


def _layer_tail(y_f, s_lru, p_lru, h_next, p_g, y_n, x, mod, w_out, g_ffn, w_gate, w_up, w_down, l,
                nxt=None, g_final=None, splits=None):
    b, s, d = x.shape
    tm = TOKEN_TILE
    d_ff = w_gate.shape[-1]
    assert d_ff % FF_CHUNK == 0 and tm % GRID_W == 0 and tm % LRU_TILE == 0
    nt = s // tm
    n_tiles = b * nt
    cur = lambda k: jnp.minimum(k, n_tiles - 1)
    prv = lambda k: jnp.maximum(k - 1, 0)
    tok_in = lambda w: pl.BlockSpec((1, tm, w), lambda k: (cur(k) // nt, cur(k) % nt, 0))
    mod_at = lambda layer, sel: pl.BlockSpec((None, 1) + mod.shape[2:],
                                             lambda k: (layer, sel(k) // nt, 0, 0))
    in_specs = [pl.BlockSpec((1, y_f.shape[1], tm // GRID_W * ROW_PITCH, LANES),
                             lambda k: (cur(k) // nt, 0, cur(k) % nt, 0)),
                tok_in(s_lru.shape[-1]), tok_in(p_lru.shape[-1]),
                pl.BlockSpec((1, 1) + h_next.shape[2:], lambda k: (cur(k) // nt, cur(k) % nt, 0, 0)),
                tok_in(p_g.shape[-1]), tok_in(y_n.shape[-1]), tok_in(d),
                mod_at(l, cur), mod_at(l, prv), _layer_spec(w_out, l), _layer_spec(g_ffn, l),
                _layer_spec(w_gate, l), _layer_spec(w_up, l), _layer_spec(w_down, l)]
    args = [y_f, s_lru, p_lru, h_next, p_g, y_n, x, mod, mod, w_out, g_ffn, w_gate, w_up, w_down]
    out_shape = [jax.ShapeDtypeStruct((b, s, d), F32)]
    out_specs = [pl.BlockSpec((1, tm, d), lambda k: (prv(k) // nt, prv(k) % nt, 0))]
    if nxt is None:
        in_specs += [_const_spec(g_final.shape)]
        args += [g_final]
        splits = None
    else:
        g_mix, w_in = nxt
        in_specs += [mod_at(l + 1, prv), _layer_spec(g_mix, l + 1), _layer_spec(w_in, l + 1)]
        args += [mod, g_mix, w_in]
        shapes, specs = _in_proj_out(b, s, tm, splits, lambda k: (prv(k) // nt, prv(k) % nt))
        out_shape += shapes
        out_specs += specs
    return pl.pallas_call(
        functools.partial(_tail_kernel, d_ff=d_ff, splits=splits),
        grid=(n_tiles + 1,),
        in_specs=in_specs,
        out_specs=out_specs,
        out_shape=out_shape,
        scratch_shapes=[pltpu.VMEM((tm, d_ff), BF16),
                        pltpu.VMEM((tm, d), F32),
                        pltpu.VMEM((tm, d), BF16)],
        compiler_params=pltpu.CompilerParams(dimension_semantics=("arbitrary",),
                                             vmem_limit_bytes=TAIL_VMEM_LIMIT),
        name="layer_tail",
    )(*args)


def kernel(x, c, w_ada, b_ada, g_mix, g_ffn, w_in, w_fourier, conv_w, conv_b, lru_w_a, lru_b_a,
           lru_w_x, lru_b_x, lru_lambda, na_rpb, w_out, w_ffn_gate, w_ffn_up, w_ffn_down, g_final):
    b, s, d = x.shape
    depth = w_in.shape[0]
    n_groups = w_fourier.shape[1]
    d_f = n_groups * HEAD_DIM
    d_l = conv_w.shape[-1]
    d_n = na_rpb.shape[1] * HEAD_DIM
    tm = TOKEN_TILE

    mod = _modulation(c, w_ada, b_ada).reshape(depth, b, N_MOD, d)
    tables = _fourier_tables(n_groups)
    splits = _in_proj_splits(d_f, d_l, d_n)
    w_in_b = w_in.astype(BF16)
    w_out_b = w_out.astype(BF16)
    w_gate_b = w_ffn_gate.astype(BF16)
    w_up_b = w_ffn_up.astype(BF16)
    w_down_b = w_ffn_down.astype(BF16)
    g_mix3 = g_mix.reshape(depth, 1, d)
    g_ffn3 = g_ffn.reshape(depth, 1, d)
    w_fourier_b = _block_diag(w_fourier).astype(BF16)
    w_gates = _gate_weights(lru_w_a, lru_b_a, lru_w_x, lru_b_x)
    conv_b3 = conv_b.reshape(depth, 1, d_l)
    na_bias = _na_bias(na_rpb)

    p_f, p_x, p_g, q, k, v = _in_proj(x, mod, g_mix3, w_in_b, 0, splits)
    for l in range(depth):
        y_f = _fourier_mix(p_f, w_fourier_b, l, tables)
        s_lru, p_lru, h_next = _rg_lru(p_x, conv_w, conv_b3, w_gates, lru_lambda, l)
        h_next = jnp.transpose(h_next, (1, 0, 2)).reshape(b, s // tm, tm // LRU_TILE, d_l)
        y_n = _neighbourhood_attention(q, k, v, na_bias, l)
        common = (y_f, s_lru, p_lru, h_next, p_g, y_n, x, mod, w_out_b, g_ffn3,
                  w_gate_b, w_up_b, w_down_b, l)
        if l + 1 < depth:
            x, p_f, p_x, p_g, q, k, v = _layer_tail(*common, nxt=(g_mix3, w_in_b), splits=splits)
        else:
            (x,) = _layer_tail(*common, g_final=g_final.reshape(1, d))
    return x
```
